```python
import math
import jax, jax.numpy as jnp
from jax import lax
import numpy as np

D_MODEL = 1024
BATCH = 2
SEQ = 16384
DEPTH = 4
DEC_BATCH = 32
DEC_SEQ = 2048
PAST_LEN = 128

N_MIXERS = 2
N_ATTN_LAYERS = (DEPTH + 1) // 2
N_MLSTM_LAYERS = DEPTH // 2

ATTN_HEADS = 16
ATTN_KV_HEADS = 4
ATTN_GROUP = ATTN_HEADS // ATTN_KV_HEADS
ATTN_HEAD_DIM = D_MODEL // ATTN_HEADS
ATTN_WIDTH = ATTN_HEADS * ATTN_HEAD_DIM
ATTN_KV_WIDTH = ATTN_KV_HEADS * ATTN_HEAD_DIM
ATTN_IN = 2 * ATTN_WIDTH + 2 * ATTN_KV_WIDTH
WINDOW = 128
BLOCK = 128
ROPE_THETA = 10000.0

MLSTM_HEADS = 4
MLSTM_V_DIM = D_MODEL // MLSTM_HEADS
MLSTM_QK_DIM = MLSTM_V_DIM // 2
MLSTM_WIDTH = MLSTM_HEADS * MLSTM_V_DIM
MLSTM_QK_WIDTH = MLSTM_HEADS * MLSTM_QK_DIM
MLSTM_N_GATES = 4 * MLSTM_HEADS
MLSTM_IN = 2 * MLSTM_QK_WIDTH + 3 * MLSTM_WIDTH + MLSTM_N_GATES
CHUNK = 64

EPS = 1e-6
NEG_INIT = -1e30

kernel_name = 'hybrid_swa_mlstm_bidir_encoder'


def rms_norm(x, g):
    xf = x.astype(jnp.float32)
    y = xf * lax.rsqrt(jnp.mean(xf * xf, axis=-1, keepdims=True) + EPS)
    return (y * g.astype(jnp.float32)).astype(x.dtype)


def rope(x, pos):
    half = x.shape[-1] // 2
    inv = jnp.exp(-math.log(ROPE_THETA) * jnp.arange(half, dtype=jnp.float32) / half)
    ang = pos.astype(jnp.float32)[:, None] * inv[None, :]
    cos = jnp.cos(ang)[None, :, None, :]
    sin = jnp.sin(ang)[None, :, None, :]
    xf = x.astype(jnp.float32)
    x1, x2 = xf[..., :half], xf[..., half:]
    return jnp.concatenate([x1 * cos - x2 * sin, x2 * cos + x1 * sin], axis=-1).astype(x.dtype)


def window_attention_mixer(xn, w_in, sink, w_out):
    B, S, _ = xn.shape
    NB = S // BLOCK
    proj = xn @ w_in
    q, k, v, z = jnp.split(proj, [ATTN_WIDTH, ATTN_WIDTH + ATTN_KV_WIDTH, ATTN_WIDTH + 2 * ATTN_KV_WIDTH], axis=-1)
    pos = jnp.arange(S)
    q = rope(q.reshape(B, S, ATTN_HEADS, ATTN_HEAD_DIM), pos)
    k = rope(k.reshape(B, S, ATTN_KV_HEADS, ATTN_HEAD_DIM), pos)
    v = v.reshape(B, S, ATTN_KV_HEADS, ATTN_HEAD_DIM)
    qb = q.reshape(B, NB, BLOCK, ATTN_KV_HEADS, ATTN_GROUP, ATTN_HEAD_DIM)

    def band(t):
        tp = jnp.pad(t, ((0, 0), (BLOCK, BLOCK), (0, 0), (0, 0)))
        tp = tp.reshape(B, NB + 2, BLOCK, ATTN_KV_HEADS, ATTN_HEAD_DIM)
        return jnp.concatenate([tp[:, :-2], tp[:, 1:-1], tp[:, 2:]], axis=2)

    kb, vb = band(k), band(v)
    s = jnp.einsum('bnqgrd,bnkgd->bngrqk', qb, kb, preferred_element_type=jnp.float32)
    s = s * (ATTN_HEAD_DIM ** -0.5)
    a = jnp.arange(BLOCK)
    c = jnp.arange(3 * BLOCK)
    in_band = jnp.abs(c[None, :] - BLOCK - a[:, None]) <= WINDOW
    key_pos = jnp.arange(NB)[:, None] * BLOCK - BLOCK + c[None, :]
    in_seq = (key_pos >= 0) & (key_pos < S)
    mask = in_band[None, :, :] & in_seq[:, None, :]
    s = jnp.where(mask[None, :, None, None], s, -jnp.inf)
    sk = sink.astype(jnp.float32).reshape(ATTN_KV_HEADS, ATTN_GROUP)[None, None, :, :, None, None]
    m = jnp.maximum(jnp.max(s, axis=-1, keepdims=True), sk)
    p = jnp.exp(s - m)
    den = jnp.sum(p, axis=-1, keepdims=True) + jnp.exp(sk - m)
    p = (p / den).astype(vb.dtype)
    o = jnp.einsum('bngrqk,bnkgd->bnqgrd', p, vb).reshape(B, S, ATTN_WIDTH)
    return (o * jax.nn.silu(z)) @ w_out


def mlstm_chunkwise(q, k, v, ig, fg):
    B, S, H, DQK = q.shape
    DV = v.shape[-1]
    NC = S // CHUNK

    def chunks(t):
        return t.astype(jnp.float32).reshape(B, NC, CHUNK, H, -1).transpose(0, 3, 1, 2, 4)

    qc, kc, vc = chunks(q), chunks(k), chunks(v)
    igc = ig.reshape(B, NC, CHUNK, H).transpose(0, 3, 1, 2)
    logf = jax.nn.log_sigmoid(fg.reshape(B, NC, CHUNK, H).transpose(0, 3, 1, 2))
    b = jnp.cumsum(logf, axis=-1)
    b_end = b[..., -1]
    tri = jnp.tril(jnp.ones((CHUNK, CHUNK), dtype=bool))
    D = jnp.where(tri, b[..., :, None] - b[..., None, :] + igc[..., None, :], -jnp.inf)

    a = b_end[..., None] - b + igc
    m_loc = jnp.max(a, axis=-1)
    w = jnp.exp(a - m_loc[..., None])
    C_loc = jnp.einsum('bhcsk,bhcsv->bhckv', w[..., None] * kc, vc)
    n_loc = jnp.einsum('bhcs,bhcsk->bhck', w, kc)

    def step(carry, inp):
        C, n, m = carry
        C_l, n_l, m_l, be = inp
        m_new = jnp.maximum(be + m, m_l)
        sp = jnp.exp(be + m - m_new)
        sl = jnp.exp(m_l - m_new)
        C_new = sp[..., None, None] * C + sl[..., None, None] * C_l
        n_new = sp[..., None] * n + sl[..., None] * n_l
        return (C_new, n_new, m_new), (C, n, m)

    xs = (jnp.moveaxis(C_loc, 2, 0), jnp.moveaxis(n_loc, 2, 0), jnp.moveaxis(m_loc, 2, 0), jnp.moveaxis(b_end, 2, 0))
    init = (jnp.zeros((B, H, DQK, DV), jnp.float32), jnp.zeros((B, H, DQK), jnp.float32),
            jnp.full((B, H), NEG_INIT, jnp.float32))
    _, (Cs, ns, ms) = lax.scan(step, init, xs)
    Cs = jnp.moveaxis(Cs, 0, 2)
    ns = jnp.moveaxis(ns, 0, 2)
    ms = jnp.moveaxis(ms, 0, 2)

    g_prev = b + ms[..., None]
    m_t = jnp.maximum(jnp.max(D, axis=-1), g_prev)
    P = jnp.exp(D - m_t[..., None]) * jnp.einsum('bhctk,bhcsk->bhcts', qc, kc)
    sc = jnp.exp(g_prev - m_t)
    num = jnp.einsum('bhcts,bhcsv->bhctv', P, vc) + sc[..., None] * jnp.einsum('bhctk,bhckv->bhctv', qc, Cs)
    den = jnp.sum(P, axis=-1) + sc * jnp.einsum('bhctk,bhck->bhct', qc, ns)
    h = num / jnp.maximum(jnp.abs(den), jnp.exp(-m_t))[..., None]
    return h.transpose(0, 2, 3, 1, 4).reshape(B, S, H, DV)


def mlstm_mixer(xn, w_in, gate_bias, head_norm, w_out):
    B, S, _ = xn.shape
    cuts = [MLSTM_QK_WIDTH, 2 * MLSTM_QK_WIDTH, 2 * MLSTM_QK_WIDTH + MLSTM_WIDTH,
            2 * MLSTM_QK_WIDTH + 2 * MLSTM_WIDTH, 2 * MLSTM_QK_WIDTH + 3 * MLSTM_WIDTH]
    proj = xn @ w_in
    q, k, v, o, z, g = jnp.split(proj, cuts, axis=-1)
    q = q.reshape(B, S, MLSTM_HEADS, MLSTM_QK_DIM)
    k = k.reshape(B, S, MLSTM_HEADS, MLSTM_QK_DIM) * (MLSTM_QK_DIM ** -0.5)
    v = v.reshape(B, S, MLSTM_HEADS, MLSTM_V_DIM)
    g = (g.astype(jnp.float32) + gate_bias.astype(jnp.float32)).reshape(B, S, 4, MLSTM_HEADS)
    ig_f, fg_f, ig_b, fg_b = g[:, :, 0], g[:, :, 1], g[:, :, 2], g[:, :, 3]
    h_f = mlstm_chunkwise(q, k, v, ig_f, fg_f)
    flip = lambda t: jnp.flip(t, axis=1)
    h_b = flip(mlstm_chunkwise(flip(q), flip(k), flip(v), flip(ig_b), flip(fg_b)))
    h = h_f + h_b
    h = h * lax.rsqrt(jnp.mean(h * h, axis=-1, keepdims=True) + EPS)
    h = h * head_norm.astype(jnp.float32).reshape(MLSTM_HEADS, MLSTM_V_DIM)
    h = h.reshape(B, S, MLSTM_WIDTH).astype(xn.dtype)
    h = h * jax.nn.sigmoid(o) * jax.nn.silu(z)
    return h @ w_out


def trunk(x, norm_g, attn_w_in, attn_sink, attn_w_out, mlstm_w_in, mlstm_gate_bias, mlstm_head_norm, mlstm_w_out, final_norm_g):
    for i in range(DEPTH):
        j = i // N_MIXERS
        xn = rms_norm(x, norm_g[i])
        if i % N_MIXERS == 0:
            x = x + window_attention_mixer(xn, attn_w_in[j], attn_sink[j], attn_w_out[j])
        else:
            x = x + mlstm_mixer(xn, mlstm_w_in[j], mlstm_gate_bias[j], mlstm_head_norm[j], mlstm_w_out[j])
    return rms_norm(x, final_norm_g)


def setup_inputs(seed: int = 0) -> dict:
    key = jax.random.key(seed)
    ks = jax.random.split(key, 14)
    f32 = jnp.float32
    x_prompt = jax.random.normal(ks[0], (BATCH, SEQ, D_MODEL), f32)
    x_sample = jax.random.normal(ks[1], (DEC_BATCH, DEC_SEQ, D_MODEL), f32)
    norm_g = 1.0 + 0.02 * jax.random.normal(ks[2], (DEPTH, D_MODEL), f32)
    attn_w_in = jax.random.normal(ks[3], (N_ATTN_LAYERS, D_MODEL, ATTN_IN), f32) * D_MODEL ** -0.5
    attn_sink = 0.5 * jax.random.normal(ks[4], (N_ATTN_LAYERS, ATTN_HEADS), f32)
    attn_w_out = jax.random.normal(ks[5], (N_ATTN_LAYERS, ATTN_WIDTH, D_MODEL), f32) * ATTN_WIDTH ** -0.5
    mlstm_w_in = jax.random.normal(ks[6], (N_MLSTM_LAYERS, D_MODEL, MLSTM_IN), f32) * D_MODEL ** -0.5
    ig_bias = 0.1 * jax.random.normal(ks[7], (N_MLSTM_LAYERS, 2, MLSTM_HEADS), f32)
    fg_bias = jnp.linspace(3.0, 6.0, MLSTM_HEADS, dtype=f32)[None, None, :] + 0.1 * jax.random.normal(ks[8], (N_MLSTM_LAYERS, 2, MLSTM_HEADS), f32)
    mlstm_gate_bias = jnp.stack([ig_bias[:, 0], fg_bias[:, 0], ig_bias[:, 1], fg_bias[:, 1]], axis=1).reshape(N_MLSTM_LAYERS, MLSTM_N_GATES)
    mlstm_head_norm = 1.0 + 0.02 * jax.random.normal(ks[9], (N_MLSTM_LAYERS, MLSTM_WIDTH), f32)
    mlstm_w_out = jax.random.normal(ks[10], (N_MLSTM_LAYERS, MLSTM_WIDTH, D_MODEL), f32) * MLSTM_WIDTH ** -0.5
    final_norm_g = 1.0 + 0.02 * jax.random.normal(ks[11], (D_MODEL,), f32)
    return {'x_prompt': x_prompt, 'x_sample': x_sample, 'norm_g': norm_g,
            'attn_w_in': attn_w_in, 'attn_sink': attn_sink, 'attn_w_out': attn_w_out,
            'mlstm_w_in': mlstm_w_in, 'mlstm_gate_bias': mlstm_gate_bias,
            'mlstm_head_norm': mlstm_head_norm, 'mlstm_w_out': mlstm_w_out,
            'final_norm_g': final_norm_g}


def reference(x_prompt, x_sample, norm_g, attn_w_in, attn_sink, attn_w_out, mlstm_w_in, mlstm_gate_bias, mlstm_head_norm, mlstm_w_out, final_norm_g):
    y_prompt = trunk(x_prompt, norm_g, attn_w_in, attn_sink, attn_w_out, mlstm_w_in, mlstm_gate_bias, mlstm_head_norm, mlstm_w_out, final_norm_g)
    y_sample = trunk(x_sample, norm_g, attn_w_in, attn_sink, attn_w_out, mlstm_w_in, mlstm_gate_bias, mlstm_head_norm, mlstm_w_out, final_norm_g)
    return (y_prompt, y_sample)
```

```python
import functools
import math

import jax
import jax.numpy as jnp
from jax import lax
from jax.experimental import pallas as pl
from jax.experimental.pallas import tpu as pltpu

F32 = jnp.float32
BF16 = jnp.bfloat16

D_MODEL = 1024
EPS = 1e-6
NEG = -1e30

ATTN_HEADS = 16
ATTN_KV_HEADS = 4
ATTN_HEAD_DIM = 64
ATTN_WIDTH = ATTN_HEADS * ATTN_HEAD_DIM
ATTN_KV_WIDTH = ATTN_KV_HEADS * ATTN_HEAD_DIM
WINDOW = 128
ROPE_THETA = 10000.0
KEY_BLOCK = 128

MLSTM_HEADS = 4
MLSTM_V_DIM = 256
MLSTM_QK_DIM = 128
MLSTM_WIDTH = MLSTM_HEADS * MLSTM_V_DIM
MLSTM_QK_WIDTH = MLSTM_HEADS * MLSTM_QK_DIM
MLSTM_MAIN_IN = 2 * MLSTM_QK_WIDTH + 3 * MLSTM_WIDTH
NEG_INIT = -1e30
CHUNK = 128
GATE_SLAB = 8

LANES = 128
VMEM_LIMIT = 56 * 1024 * 1024

PROJ_TILE = 512
ATTN_TILE = 512
MLSTM_TILE = 512


def _params(n_axes):
    return pltpu.CompilerParams(dimension_semantics=("arbitrary",) * n_axes, vmem_limit_bytes=VMEM_LIMIT)


def _normed_bf16(x_ref, g_ref):
    x = x_ref[...]
    ms = jnp.mean(x * x, axis=-1, keepdims=True)
    return (x * lax.rsqrt(ms + EPS) * g_ref[...]).astype(BF16)


def _attn_in_kernel(x_ref, g_ref, w_ref, cos_ref, sin_ref, q_ref, kv_ref, gate_ref, xn_scr):
    tm = x_ref.shape[0]
    xn_scr[...] = _normed_bf16(x_ref, g_ref)
    cos = cos_ref[...]
    sin = sin_ref[...]
    lane = lax.broadcasted_iota(jnp.int32, (tm, LANES), 1)
    first_half = (lane % ATTN_HEAD_DIM) < (ATTN_HEAD_DIM // 2)

    def rope(a):
        partner = jnp.where(first_half, pltpu.roll(a, LANES - 32, 1), pltpu.roll(a, 32, 1))
        return a * cos + partner * sin

    def proj(col, width):
        return jnp.dot(xn_scr[...], w_ref[:, col:col + width], preferred_element_type=F32)

    scale = ATTN_HEAD_DIM ** -0.5
    for c in range(ATTN_WIDTH // 256):
        acc = proj(256 * c, 256)
        for j in range(2):
            q_ref[:, 256 * c + LANES * j:256 * c + LANES * (j + 1)] = (
                rope(acc[:, LANES * j:LANES * (j + 1)]) * scale).astype(BF16)
    acc = proj(ATTN_WIDTH, 2 * ATTN_KV_WIDTH)
    for j in range(ATTN_KV_WIDTH // LANES):
        kv_ref[:, LANES * j:LANES * (j + 1)] = rope(acc[:, LANES * j:LANES * (j + 1)]).astype(BF16)
    kv_ref[:, ATTN_KV_WIDTH:] = acc[:, ATTN_KV_WIDTH:].astype(BF16)
    z0 = ATTN_WIDTH + 2 * ATTN_KV_WIDTH
    for c in range(ATTN_WIDTH // 256):
        z = proj(z0 + 256 * c, 256)
        gate_ref[:, 256 * c:256 * (c + 1)] = (z * jax.nn.sigmoid(z)).astype(BF16)


def _attn_in(x, g, w, cos_t, sin_t, seq):
    n = x.shape[0]
    tm = min(PROJ_TILE, seq)
    tiles_per_seq = seq // tm
    row = lambda i: (i, 0)
    const = lambda i: (0, 0)
    pos = lambda i: (i % tiles_per_seq, 0)
    return pl.pallas_call(
        _attn_in_kernel,
        grid=(n // tm,),
        in_specs=[pl.BlockSpec((tm, D_MODEL), row), pl.BlockSpec((1, D_MODEL), const),
                  pl.BlockSpec(w.shape, const), pl.BlockSpec((tm, LANES), pos), pl.BlockSpec((tm, LANES), pos)],
        out_specs=[pl.BlockSpec((tm, ATTN_WIDTH), row), pl.BlockSpec((tm, 2 * ATTN_KV_WIDTH), row),
                   pl.BlockSpec((tm, ATTN_WIDTH), row)],
        out_shape=[jax.ShapeDtypeStruct((n, ATTN_WIDTH), BF16), jax.ShapeDtypeStruct((n, 2 * ATTN_KV_WIDTH), BF16),
                   jax.ShapeDtypeStruct((n, ATTN_WIDTH), BF16)],
        scratch_shapes=[pltpu.VMEM((tm, D_MODEL), BF16)],
        compiler_params=_params(1),
        name="attn_in",
    )(x, g, w, cos_t, sin_t)


def _attn_core_kernel(sink_ref, q_ref, kv_ref, kvp_ref, kvn_ref, gate_ref, x_ref, wo_ref, out_ref,
                      kx_scr, vx_scr, o_scr):
    t = pl.program_id(1)
    n_t = pl.num_programs(1)
    tq = q_ref.shape[0]
    nblk = tq // KEY_BLOCK
    kb = KEY_BLOCK
    half = ATTN_HEAD_DIM

    lane = lax.broadcasted_iota(jnp.int32, (kb, LANES), 1)
    low = lane < half
    ones_lo = jnp.where(low, 1.0, 0.0).astype(BF16)
    ones_hi = jnp.where(low, 0.0, 1.0).astype(BF16)

    def expand(blk_ref, r0, jb):
        for vi in range(ATTN_KV_WIDTH // LANES):
            kcol = blk_ref[r0:r0 + kb, LANES * vi:LANES * (vi + 1)].astype(F32)
            vcol = blk_ref[r0:r0 + kb, ATTN_KV_WIDTH + LANES * vi:ATTN_KV_WIDTH + LANES * (vi + 1)].astype(F32)
            kswap = pltpu.roll(kcol, half, 1)
            vswap = pltpu.roll(vcol, half, 1)
            for hf in range(2):
                g = 2 * vi + hf
                k_src_lo, k_src_hi = (kcol, kswap) if hf == 0 else (kswap, kcol)
                v_src_lo, v_src_hi = (vcol, vswap) if hf == 0 else (vswap, vcol)
                kx_scr[g, jb, 0] = jnp.where(low, k_src_lo, 0.0).astype(BF16)
                kx_scr[g, jb, 1] = jnp.where(low, 0.0, k_src_hi).astype(BF16)
                vx_scr[g, jb, 0, :, :LANES] = jnp.where(low, v_src_lo, 0.0).astype(BF16)
                vx_scr[g, jb, 0, :, LANES:] = ones_lo
                vx_scr[g, jb, 1, :, :LANES] = jnp.where(low, 0.0, v_src_hi).astype(BF16)
                vx_scr[g, jb, 1, :, LANES:] = ones_hi

    expand(kvp_ref, 0, 0)
    for jb in range(nblk):
        expand(kv_ref, kb * jb, jb + 1)
    expand(kvn_ref, 0, nblk + 1)

    qi = lax.broadcasted_iota(jnp.int32, (kb, kb), 0)
    ki = lax.broadcasted_iota(jnp.int32, (kb, kb), 1)
    band_prev = jnp.where(ki >= qi, 0.0, NEG).astype(F32)
    band_next = jnp.where(ki <= qi, 0.0, NEG).astype(F32)

    def block_body(blk, carry):
        r0 = pl.multiple_of(blk * kb, kb)
        seq_start = jnp.logical_and(t == 0, blk == 0)
        seq_end = jnp.logical_and(t == n_t - 1, blk == nblk - 1)
        bias_prev = jnp.where(seq_start, NEG, band_prev)
        bias_next = jnp.where(seq_end, NEG, band_next)
        for g in range(ATTN_KV_HEADS):
            c0 = 2 * LANES * g
            q2 = jnp.concatenate([q_ref[pl.ds(r0, kb), c0:c0 + LANES],
                                  q_ref[pl.ds(r0, kb), c0 + LANES:c0 + 2 * LANES]], axis=0)
            kx = kx_scr[g, pl.ds(blk, 3)].reshape(6 * kb, LANES)
            s = lax.dot_general(q2, kx, (((1,), (1,)), ((), ())), preferred_element_type=F32)
            p_rows = []
            sink_terms = []
            for p in range(2):
                cols = [None] * 6
                es = []
                for ab in range(2):
                    head = 4 * g + 2 * p + ab
                    sink = sink_ref[head]
                    sp = s[kb * p:kb * (p + 1), kb * ab:kb * (ab + 1)] + bias_prev
                    so = s[kb * p:kb * (p + 1), kb * (2 + ab):kb * (3 + ab)]
                    sn = s[kb * p:kb * (p + 1), kb * (4 + ab):kb * (5 + ab)] + bias_next
                    m = jnp.max(jnp.maximum(jnp.maximum(sp, so), sn), axis=1, keepdims=True)
                    m = jnp.maximum(m, sink)
                    cols[ab] = jnp.exp(sp - m).astype(BF16)
                    cols[2 + ab] = jnp.exp(so - m).astype(BF16)
                    cols[4 + ab] = jnp.exp(sn - m).astype(BF16)
                    es.append(jnp.exp(sink - m))
                p_rows.append(jnp.concatenate(cols, axis=1))
                sink_terms.append(jnp.where(low, es[0], es[1]))
            pmat = jnp.concatenate(p_rows, axis=0)
            vx = vx_scr[g, pl.ds(blk, 3)].reshape(6 * kb, 2 * LANES)
            r = jnp.dot(pmat, vx, preferred_element_type=F32)
            for p in range(2):
                num = r[kb * p:kb * (p + 1), :LANES]
                den = r[kb * p:kb * (p + 1), LANES:] + sink_terms[p]
                o_scr[pl.ds(r0, kb), c0 + LANES * p:c0 + LANES * (p + 1)] = num / den
        return carry

    lax.fori_loop(0, nblk, block_body, 0)

    y = (o_scr[...] * gate_ref[...].astype(F32)).astype(BF16)
    out_ref[...] = x_ref[...] + jnp.dot(y, wo_ref[...], preferred_element_type=F32)


def _attn_core(sink, q, kv, gate, x, wo, seq):
    n = x.shape[0]
    tq = min(ATTN_TILE, seq)
    n_t = seq // tq
    nblk = tq // KEY_BLOCK
    last_blk = n // KEY_BLOCK - 1
    row = lambda b, t: (b * n_t + t, 0)
    const = lambda b, t: (0, 0)
    prev = lambda b, t: (jnp.maximum((b * n_t + t) * nblk - 1, 0), 0)
    nxt = lambda b, t: (jnp.minimum((b * n_t + t + 1) * nblk, last_blk), 0)
    return pl.pallas_call(
        _attn_core_kernel,
        grid=(n // seq, n_t),
        in_specs=[pl.BlockSpec(memory_space=pltpu.SMEM),
                  pl.BlockSpec((tq, ATTN_WIDTH), row), pl.BlockSpec((tq, 2 * ATTN_KV_WIDTH), row),
                  pl.BlockSpec((KEY_BLOCK, 2 * ATTN_KV_WIDTH), prev), pl.BlockSpec((KEY_BLOCK, 2 * ATTN_KV_WIDTH), nxt),
                  pl.BlockSpec((tq, ATTN_WIDTH), row), pl.BlockSpec((tq, D_MODEL), row),
                  pl.BlockSpec((ATTN_WIDTH, D_MODEL), const)],
        out_specs=pl.BlockSpec((tq, D_MODEL), row),
        out_shape=jax.ShapeDtypeStruct((n, D_MODEL), F32),
        scratch_shapes=[pltpu.VMEM((ATTN_KV_HEADS, nblk + 2, 2, KEY_BLOCK, LANES), BF16),
                        pltpu.VMEM((ATTN_KV_HEADS, nblk + 2, 2, KEY_BLOCK, 2 * LANES), BF16),
                        pltpu.VMEM((tq, ATTN_WIDTH), F32)],
        compiler_params=_params(2),
        name="attn_core",
    )(sink, q, kv, kv, kv, gate, x, wo)


def _log_sigmoid(x):
    return jnp.minimum(x, 0.0) - jnp.log1p(jnp.exp(-jnp.abs(x)))


def _segment_scan(x, op, fill, pos_in_chunk, reverse):
    width = x.shape[1]
    d = 1
    while d < CHUNK:
        if reverse:
            shifted = pltpu.roll(x, width - d, 1)
            valid = pos_in_chunk < CHUNK - d
        else:
            shifted = pltpu.roll(x, d, 1)
            valid = pos_in_chunk >= d
        x = op(x, jnp.where(valid, shifted, fill))
        d *= 2
    return x


def _mlstm_in_kernel(x_ref, g_ref, w_ref, wg_ref, bias_ref, q_ref, k_ref, v_ref, og_ref, colg_ref, rowg_ref, xn_scr):
    tm = x_ref.shape[0]
    xn_scr[...] = _normed_bf16(x_ref, g_ref)

    def proj(col, width):
        return jnp.dot(xn_scr[...], w_ref[:, col:col + width], preferred_element_type=F32)

    for c in range(MLSTM_QK_WIDTH // 256):
        q_ref[:, 256 * c:256 * (c + 1)] = proj(256 * c, 256).astype(BF16)
    kscale = MLSTM_QK_DIM ** -0.5
    for c in range(MLSTM_QK_WIDTH // 256):
        k_ref[:, 256 * c:256 * (c + 1)] = (proj(MLSTM_QK_WIDTH + 256 * c, 256) * kscale).astype(BF16)
    v0 = 2 * MLSTM_QK_WIDTH
    for c in range(MLSTM_WIDTH // 256):
        v_ref[:, 256 * c:256 * (c + 1)] = proj(v0 + 256 * c, 256).astype(BF16)
    o0 = v0 + MLSTM_WIDTH
    z0 = o0 + MLSTM_WIDTH
    for c in range(MLSTM_WIDTH // 256):
        o = proj(o0 + 256 * c, 256)
        z = proj(z0 + 256 * c, 256)
        og_ref[:, 256 * c:256 * (c + 1)] = (jax.nn.sigmoid(o) * (z * jax.nn.sigmoid(z))).astype(BF16)

    gates = jnp.dot(xn_scr[...], wg_ref[...], preferred_element_type=F32) + bias_ref[...]
    gt = gates.T
    s = GATE_SLAB
    ig_f, fg_f, ig_b, fg_b = gt[0:s], gt[s:2 * s], gt[2 * s:3 * s], gt[3 * s:4 * s]
    pos = lax.broadcasted_iota(jnp.int32, (s, tm), 1) % CHUNK
    b_f = _segment_scan(_log_sigmoid(fg_f), jnp.add, 0.0, pos, False)
    u_f = ig_f - b_f
    cu_f = _segment_scan(u_f, jnp.maximum, -jnp.inf, pos, False)
    b_b = _segment_scan(_log_sigmoid(fg_b), jnp.add, 0.0, pos, True)
    u_b = ig_b - b_b
    cu_b = _segment_scan(u_b, jnp.maximum, -jnp.inf, pos, True)
    pad = jnp.zeros((LANES - 6 * s, tm), F32)
    colg_ref[...] = jnp.concatenate([b_f, u_f, cu_f, b_b, u_b, cu_b, pad], axis=0).T
    rowg_ref[...] = jnp.concatenate([u_f, u_b], axis=0)


def _mlstm_in(x, g, w, wg, bias, seq):
    n = x.shape[0]
    tm = min(PROJ_TILE, seq)
    row = lambda i: (i, 0)
    const = lambda i: (0, 0)
    col = lambda i: (0, i)
    return pl.pallas_call(
        _mlstm_in_kernel,
        grid=(n // tm,),
        in_specs=[pl.BlockSpec((tm, D_MODEL), row), pl.BlockSpec((1, D_MODEL), const),
                  pl.BlockSpec(w.shape, const), pl.BlockSpec(wg.shape, const), pl.BlockSpec((1, LANES), const)],
        out_specs=[pl.BlockSpec((tm, MLSTM_QK_WIDTH), row), pl.BlockSpec((tm, MLSTM_QK_WIDTH), row),
                   pl.BlockSpec((tm, MLSTM_WIDTH), row), pl.BlockSpec((tm, MLSTM_WIDTH), row),
                   pl.BlockSpec((tm, LANES), row), pl.BlockSpec((2 * GATE_SLAB, tm), col)],
        out_shape=[jax.ShapeDtypeStruct((n, MLSTM_QK_WIDTH), BF16), jax.ShapeDtypeStruct((n, MLSTM_QK_WIDTH), BF16),
                   jax.ShapeDtypeStruct((n, MLSTM_WIDTH), BF16), jax.ShapeDtypeStruct((n, MLSTM_WIDTH), BF16),
                   jax.ShapeDtypeStruct((n, LANES), F32), jax.ShapeDtypeStruct((2 * GATE_SLAB, n), F32)],
        scratch_shapes=[pltpu.VMEM((tm, D_MODEL), BF16)],
        compiler_params=_params(1),
        name="mlstm_in",
    )(x, g, w, wg, bias)


def _mlstm_sweep_tile(q_ref, k_ref, v_ref, colg_ref, rowg_ref, c_scr, n_scr, m_scr, h_scr, reverse):
    tt = q_ref.shape[0]
    nch = tt // CHUNK
    L = CHUNK
    ti = lax.broadcasted_iota(jnp.int32, (L, L), 0)
    si = lax.broadcasted_iota(jnp.int32, (L, L), 1)
    causal = (si >= ti) if reverse else (si <= ti)
    lane0 = 3 * GATE_SLAB if reverse else 0
    row0 = GATE_SLAB if reverse else 0
    end = 0 if reverse else L - 1

    chunks = range(nch - 1, -1, -1) if reverse else range(nch)
    for c in chunks:
        r0 = L * c
        cg = colg_ref[r0:r0 + L, :]
        for h in range(MLSTM_HEADS):
            qh = q_ref[r0:r0 + L, MLSTM_QK_DIM * h:MLSTM_QK_DIM * (h + 1)]
            kh = k_ref[r0:r0 + L, MLSTM_QK_DIM * h:MLSTM_QK_DIM * (h + 1)]
            vh = v_ref[r0:r0 + L, MLSTM_V_DIM * h:MLSTM_V_DIM * (h + 1)]
            b = cg[:, lane0 + h:lane0 + h + 1]
            u_c = cg[:, lane0 + GATE_SLAB + h:lane0 + GATE_SLAB + h + 1]
            cu = cg[:, lane0 + 2 * GATE_SLAB + h:lane0 + 2 * GATE_SLAB + h + 1]
            u_r = rowg_ref[row0 + h:row0 + h + 1, r0:r0 + L]
            m_prev = m_scr[h][:, 0:1]
            c_prev = c_scr[h]
            n_prev = n_scr[h]

            mx = jnp.maximum(cu, m_prev)
            e = jnp.exp(jnp.where(causal, u_r - mx, NEG))
            s = lax.dot_general(qh, kh, (((1,), (1,)), ((), ())), preferred_element_type=F32)
            p = e * s
            sc = jnp.exp(m_prev - mx)
            num = jnp.dot(p.astype(BF16), vh, preferred_element_type=F32)
            num = num + sc * jnp.dot(qh, c_prev.astype(BF16), preferred_element_type=F32)
            qn = jnp.sum(qh.astype(F32) * n_prev, axis=1, keepdims=True)
            den = jnp.sum(p, axis=1, keepdims=True) + sc * qn
            denom = jnp.maximum(jnp.abs(den), jnp.exp(-(b + mx)))
            h_scr[r0:r0 + L, MLSTM_V_DIM * h:MLSTM_V_DIM * (h + 1)] = num / denom

            cu_end = cu[end:end + 1, :]
            b_end = b[end:end + 1, :]
            m_ref = jnp.maximum(m_prev, cu_end)
            wk = jnp.exp(u_c - m_ref)
            sp = jnp.exp(m_prev - m_ref)
            kw = kh.astype(F32) * wk
            c_scr[h] = sp * c_prev + jnp.dot(kw.T.astype(BF16), vh, preferred_element_type=F32)
            n_scr[h] = sp * n_prev + jnp.sum(kw, axis=0, keepdims=True)
            m_scr[h] = jnp.broadcast_to(m_ref + b_end, (1, LANES))


def _reset_state(c_scr, n_scr, m_scr):
    c_scr[...] = jnp.zeros(c_scr.shape, F32)
    n_scr[...] = jnp.zeros(n_scr.shape, F32)
    m_scr[...] = jnp.full(m_scr.shape, NEG_INIT, F32)


def _mlstm_bwd_kernel(q_ref, k_ref, v_ref, colg_ref, rowg_ref, hb_ref, c_scr, n_scr, m_scr, h_scr):
    @pl.when(pl.program_id(1) == 0)
    def _():
        _reset_state(c_scr, n_scr, m_scr)

    _mlstm_sweep_tile(q_ref, k_ref, v_ref, colg_ref, rowg_ref, c_scr, n_scr, m_scr, h_scr, reverse=True)
    hb_ref[...] = h_scr[...]


def _mlstm_fwd_kernel(q_ref, k_ref, v_ref, colg_ref, rowg_ref, hb_ref, og_ref, hn_ref, x_ref, wo_ref, fg_ref,
                      out_ref, c_scr, n_scr, m_scr, h_scr, *, final_norm):
    @pl.when(pl.program_id(1) == 0)
    def _():
        _reset_state(c_scr, n_scr, m_scr)

    _mlstm_sweep_tile(q_ref, k_ref, v_ref, colg_ref, rowg_ref, c_scr, n_scr, m_scr, h_scr, reverse=False)

    for h in range(MLSTM_HEADS):
        cs = slice(MLSTM_V_DIM * h, MLSTM_V_DIM * (h + 1))
        hh = h_scr[:, cs] + hb_ref[:, cs]
        ms = jnp.mean(hh * hh, axis=-1, keepdims=True)
        hh = hh * lax.rsqrt(ms + EPS) * hn_ref[:, cs]
        h_scr[:, cs] = hh * og_ref[:, cs].astype(F32)
    y = x_ref[...] + jnp.dot(h_scr[...].astype(BF16), wo_ref[...], preferred_element_type=F32)
    if final_norm:
        ms = jnp.mean(y * y, axis=-1, keepdims=True)
        y = y * lax.rsqrt(ms + EPS) * fg_ref[...]
    out_ref[...] = y


def _mlstm_state_scratch(tt):
    return [pltpu.VMEM((MLSTM_HEADS, MLSTM_QK_DIM, MLSTM_V_DIM), F32),
            pltpu.VMEM((MLSTM_HEADS, 1, MLSTM_QK_DIM), F32),
            pltpu.VMEM((MLSTM_HEADS, 1, LANES), F32),
            pltpu.VMEM((tt, MLSTM_WIDTH), F32)]


def _mlstm_core(q, k, v, colg, rowg, og, hn, x, wo, fg, seq, final_norm):
    n = x.shape[0]
    tt = min(MLSTM_TILE, seq)
    n_t = seq // tt
    const = lambda b, t: (0, 0)
    fwd_row = lambda b, t: (b * n_t + t, 0)
    fwd_col = lambda b, t: (0, b * n_t + t)
    bwd_row = lambda b, t: (b * n_t + n_t - 1 - t, 0)
    bwd_col = lambda b, t: (0, b * n_t + n_t - 1 - t)

    def seq_specs(row, col):
        return [pl.BlockSpec((tt, MLSTM_QK_WIDTH), row), pl.BlockSpec((tt, MLSTM_QK_WIDTH), row),
                pl.BlockSpec((tt, MLSTM_WIDTH), row), pl.BlockSpec((tt, LANES), row),
                pl.BlockSpec((2 * GATE_SLAB, tt), col)]

    hb = pl.pallas_call(
        _mlstm_bwd_kernel,
        grid=(n // seq, n_t),
        in_specs=seq_specs(bwd_row, bwd_col),
        out_specs=pl.BlockSpec((tt, MLSTM_WIDTH), bwd_row),
        out_shape=jax.ShapeDtypeStruct((n, MLSTM_WIDTH), F32),
        scratch_shapes=_mlstm_state_scratch(tt),
        compiler_params=_params(2),
        name="mlstm_bwd",
    )(q, k, v, colg, rowg)

    return pl.pallas_call(
        functools.partial(_mlstm_fwd_kernel, final_norm=final_norm),
        grid=(n // seq, n_t),
        in_specs=seq_specs(fwd_row, fwd_col) + [
            pl.BlockSpec((tt, MLSTM_WIDTH), fwd_row), pl.BlockSpec((tt, MLSTM_WIDTH), fwd_row),
            pl.BlockSpec((1, MLSTM_WIDTH), const), pl.BlockSpec((tt, D_MODEL), fwd_row),
            pl.BlockSpec((MLSTM_WIDTH, D_MODEL), const), pl.BlockSpec((1, D_MODEL), const)],
        out_specs=pl.BlockSpec((tt, D_MODEL), fwd_row),
        out_shape=jax.ShapeDtypeStruct((n, D_MODEL), F32),
        scratch_shapes=_mlstm_state_scratch(tt),
        compiler_params=_params(2),
        name="mlstm_fwd",
    )(q, k, v, colg, rowg, hb, og, hn, x, wo, fg)


def _rope_tables(seq):
    half = ATTN_HEAD_DIM // 2
    inv = jnp.exp(-math.log(ROPE_THETA) * jnp.arange(half, dtype=F32) / half)
    ang = jnp.arange(seq).astype(F32)[:, None] * inv[None, :]
    cos = jnp.cos(ang)
    sin = jnp.sin(ang)
    cos_t = jnp.tile(cos, (1, LANES // half))
    sin_t = jnp.tile(jnp.concatenate([-sin, sin], axis=1), (1, LANES // ATTN_HEAD_DIM))
    return cos_t, sin_t


def _gate_slabs(a):
    lead = a.shape[:-1]
    a = a.reshape(lead + (4, MLSTM_HEADS))
    a = jnp.pad(a, [(0, 0)] * len(lead) + [(0, 0), (0, GATE_SLAB - MLSTM_HEADS)])
    a = a.reshape(lead + (4 * GATE_SLAB,))
    return jnp.pad(a, [(0, 0)] * len(lead) + [(0, LANES - 4 * GATE_SLAB)])


def _trunk(x, p):
    bsz, seq, _ = x.shape
    xf = x.reshape(bsz * seq, D_MODEL)
    cos_t, sin_t = _rope_tables(seq)
    for i in range(4):
        j = i // 2
        g = p["norm_g"][i][None, :]
        if i % 2 == 0:
            q, kv, gate = _attn_in(xf, g, p["attn_w_in"][j], cos_t, sin_t, seq)
            xf = _attn_core(p["attn_sink"][j], q, kv, gate, xf, p["attn_w_out"][j], seq)
        else:
            q, k, v, og, colg, rowg = _mlstm_in(xf, g, p["mlstm_w_main"][j], p["mlstm_w_gate"][j],
                                                p["mlstm_gate_bias"][j], seq)
            xf = _mlstm_core(q, k, v, colg, rowg, og, p["mlstm_head_norm"][j][None, :], xf, p["mlstm_w_out"][j],
                             p["final_norm_g"][None, :], seq, final_norm=(i == 3))
    return xf.reshape(bsz, seq, D_MODEL)


def kernel(x_prompt, x_sample, norm_g, attn_w_in, attn_sink, attn_w_out, mlstm_w_in, mlstm_gate_bias, mlstm_head_norm, mlstm_w_out, final_norm_g):
    p = {
        "norm_g": norm_g.astype(F32),
        "attn_w_in": attn_w_in.astype(BF16),
        "attn_sink": attn_sink.astype(F32),
        "attn_w_out": attn_w_out.astype(BF16),
        "mlstm_w_main": mlstm_w_in[:, :, :MLSTM_MAIN_IN].astype(BF16),
        "mlstm_w_gate": _gate_slabs(mlstm_w_in[:, :, MLSTM_MAIN_IN:]).astype(BF16),
        "mlstm_gate_bias": _gate_slabs(mlstm_gate_bias.astype(F32))[:, None, :],
        "mlstm_head_norm": mlstm_head_norm.astype(F32),
        "mlstm_w_out": mlstm_w_out.astype(BF16),
        "final_norm_g": final_norm_g.astype(F32),
    }
    return _trunk(x_prompt, p), _trunk(x_sample, p)
```

```python
import functools
import math

import jax
import jax.numpy as jnp
from jax import lax
from jax.experimental import pallas as pl
from jax.experimental.pallas import tpu as pltpu

F32 = jnp.float32
BF16 = jnp.bfloat16

D_MODEL = 1024
EPS = 1e-6
NEG = -1e30
LOG2E = math.log2(math.e)

ATTN_HEADS = 16
ATTN_KV_HEADS = 4
ATTN_HEAD_DIM = 64
ATTN_WIDTH = ATTN_HEADS * ATTN_HEAD_DIM
ATTN_KV_WIDTH = ATTN_KV_HEADS * ATTN_HEAD_DIM
WINDOW = 128
ROPE_THETA = 10000.0
KEY_BLOCK = 128

MLSTM_HEADS = 4
MLSTM_V_DIM = 256
MLSTM_QK_DIM = 128
MLSTM_WIDTH = MLSTM_HEADS * MLSTM_V_DIM
MLSTM_QK_WIDTH = MLSTM_HEADS * MLSTM_QK_DIM
MLSTM_MAIN_IN = 2 * MLSTM_QK_WIDTH + 3 * MLSTM_WIDTH
NEG_INIT = -1e30
CHUNK = 128
GATE_SLAB = 8

LANES = 128
VMEM_LIMIT = 56 * 1024 * 1024

PROJ_TILE = 512
ATTN_TILE = 512
MLSTM_TILE = 512


def _params(n_axes):
    return pltpu.CompilerParams(dimension_semantics=("arbitrary",) * n_axes, vmem_limit_bytes=VMEM_LIMIT)


def _normed_bf16(x_ref, g_ref):
    x = x_ref[...]
    ms = jnp.mean(x * x, axis=-1, keepdims=True)
    return (x * lax.rsqrt(ms + EPS) * g_ref[...]).astype(BF16)


def _attn_in_kernel(x_ref, g_ref, w_ref, cos_ref, sin_ref, q_ref, kv_ref, gate_ref, xn_scr):
    tm = x_ref.shape[0]
    xn_scr[...] = _normed_bf16(x_ref, g_ref)
    cos = cos_ref[...]
    sin = sin_ref[...]
    lane = lax.broadcasted_iota(jnp.int32, (tm, LANES), 1)
    first_half = (lane % ATTN_HEAD_DIM) < (ATTN_HEAD_DIM // 2)

    def rope(a):
        partner = jnp.where(first_half, pltpu.roll(a, LANES - 32, 1), pltpu.roll(a, 32, 1))
        return a * cos + partner * sin

    def proj(col, width):
        return jnp.dot(xn_scr[...], w_ref[:, col:col + width], preferred_element_type=F32)

    scale = ATTN_HEAD_DIM ** -0.5 * LOG2E
    for c in range(ATTN_WIDTH // 256):
        acc = proj(256 * c, 256)
        for j in range(2):
            q_ref[:, 256 * c + LANES * j:256 * c + LANES * (j + 1)] = (
                rope(acc[:, LANES * j:LANES * (j + 1)]) * scale).astype(BF16)
    acc = proj(ATTN_WIDTH, 2 * ATTN_KV_WIDTH)
    for j in range(ATTN_KV_WIDTH // LANES):
        kv_ref[:, LANES * j:LANES * (j + 1)] = rope(acc[:, LANES * j:LANES * (j + 1)]).astype(BF16)
    kv_ref[:, ATTN_KV_WIDTH:] = acc[:, ATTN_KV_WIDTH:].astype(BF16)
    z0 = ATTN_WIDTH + 2 * ATTN_KV_WIDTH
    for c in range(ATTN_WIDTH // 256):
        z = proj(z0 + 256 * c, 256)
        gate_ref[:, 256 * c:256 * (c + 1)] = (z * jax.nn.sigmoid(z)).astype(BF16)


def _attn_in(x, g, w, cos_t, sin_t, seq):
    n = x.shape[0]
    tm = min(PROJ_TILE, seq)
    tiles_per_seq = seq // tm
    row = lambda i: (i, 0)
    const = lambda i: (0, 0)
    pos = lambda i: (i % tiles_per_seq, 0)
    return pl.pallas_call(
        _attn_in_kernel,
        grid=(n // tm,),
        in_specs=[pl.BlockSpec((tm, D_MODEL), row), pl.BlockSpec((1, D_MODEL), const),
                  pl.BlockSpec(w.shape, const), pl.BlockSpec((tm, LANES), pos), pl.BlockSpec((tm, LANES), pos)],
        out_specs=[pl.BlockSpec((tm, ATTN_WIDTH), row), pl.BlockSpec((tm, 2 * ATTN_KV_WIDTH), row),
                   pl.BlockSpec((tm, ATTN_WIDTH), row)],
        out_shape=[jax.ShapeDtypeStruct((n, ATTN_WIDTH), BF16), jax.ShapeDtypeStruct((n, 2 * ATTN_KV_WIDTH), BF16),
                   jax.ShapeDtypeStruct((n, ATTN_WIDTH), BF16)],
        scratch_shapes=[pltpu.VMEM((tm, D_MODEL), BF16)],
        compiler_params=_params(1),
        name="attn_in",
    )(x, g, w, cos_t, sin_t)


def _attn_core_kernel(sink_ref, q_ref, kv_ref, kvp_ref, kvn_ref, gate_ref, x_ref, wo_ref, out_ref,
                      kx_scr, vx_scr, o_scr):
    t = pl.program_id(1)
    n_t = pl.num_programs(1)
    tq = q_ref.shape[0]
    nblk = tq // KEY_BLOCK
    kb = KEY_BLOCK
    half = ATTN_HEAD_DIM

    lane = lax.broadcasted_iota(jnp.int32, (kb, LANES), 1)
    low = lane < half
    ones_lo = jnp.where(low, 1.0, 0.0).astype(BF16)
    ones_hi = jnp.where(low, 0.0, 1.0).astype(BF16)

    def expand(blk_ref, r0, jb):
        for vi in range(ATTN_KV_WIDTH // LANES):
            kcol = blk_ref[r0:r0 + kb, LANES * vi:LANES * (vi + 1)].astype(F32)
            vcol = blk_ref[r0:r0 + kb, ATTN_KV_WIDTH + LANES * vi:ATTN_KV_WIDTH + LANES * (vi + 1)].astype(F32)
            kswap = pltpu.roll(kcol, half, 1)
            vswap = pltpu.roll(vcol, half, 1)
            for hf in range(2):
                g = 2 * vi + hf
                k_src_lo, k_src_hi = (kcol, kswap) if hf == 0 else (kswap, kcol)
                v_src_lo, v_src_hi = (vcol, vswap) if hf == 0 else (vswap, vcol)
                kx_scr[g, jb, 0] = jnp.where(low, k_src_lo, 0.0).astype(BF16)
                kx_scr[g, jb, 1] = jnp.where(low, 0.0, k_src_hi).astype(BF16)
                vx_scr[g, jb, 0, :, :LANES] = jnp.where(low, v_src_lo, 0.0).astype(BF16)
                vx_scr[g, jb, 0, :, LANES:] = ones_lo
                vx_scr[g, jb, 1, :, :LANES] = jnp.where(low, 0.0, v_src_hi).astype(BF16)
                vx_scr[g, jb, 1, :, LANES:] = ones_hi

    expand(kvp_ref, 0, 0)
    for jb in range(nblk):
        expand(kv_ref, kb * jb, jb + 1)
    expand(kvn_ref, 0, nblk + 1)

    qi = lax.broadcasted_iota(jnp.int32, (kb, kb), 0)
    ki = lax.broadcasted_iota(jnp.int32, (kb, kb), 1)
    band_prev = jnp.where(ki >= qi, 0.0, NEG).astype(F32)
    band_next = jnp.where(ki <= qi, 0.0, NEG).astype(F32)

    for blk in range(nblk):
        r0 = kb * blk
        bias_prev = jnp.where(t == 0, NEG, band_prev) if blk == 0 else band_prev
        bias_next = jnp.where(t == n_t - 1, NEG, band_next) if blk == nblk - 1 else band_next
        for g in range(ATTN_KV_HEADS):
            c0 = 2 * LANES * g
            q2 = jnp.concatenate([q_ref[r0:r0 + kb, c0:c0 + LANES],
                                  q_ref[r0:r0 + kb, c0 + LANES:c0 + 2 * LANES]], axis=0)
            kx = kx_scr[g, blk:blk + 3].reshape(6 * kb, LANES)
            s = lax.dot_general(q2, kx, (((1,), (1,)), ((), ())), preferred_element_type=F32)
            p_rows = []
            sink_terms = []
            for p in range(2):
                cols = [None] * 6
                maxes = []
                sinks = []
                for ab in range(2):
                    head = 4 * g + 2 * p + ab
                    sink = sink_ref[head] * LOG2E
                    sp = s[kb * p:kb * (p + 1), kb * ab:kb * (ab + 1)] + bias_prev
                    so = s[kb * p:kb * (p + 1), kb * (2 + ab):kb * (3 + ab)]
                    sn = s[kb * p:kb * (p + 1), kb * (4 + ab):kb * (5 + ab)] + bias_next
                    m = jnp.max(jnp.maximum(jnp.maximum(sp, so), sn), axis=1, keepdims=True)
                    m = jnp.maximum(m, sink)
                    cols[ab] = jnp.exp2(sp - m).astype(BF16)
                    cols[2 + ab] = jnp.exp2(so - m).astype(BF16)
                    cols[4 + ab] = jnp.exp2(sn - m).astype(BF16)
                    maxes.append(m)
                    sinks.append(sink)
                p_rows.append(jnp.concatenate(cols, axis=1))
                sink_terms.append(jnp.exp2(jnp.where(low, sinks[0], sinks[1]) - jnp.where(low, maxes[0], maxes[1])))
            pmat = jnp.concatenate(p_rows, axis=0)
            vx = vx_scr[g, blk:blk + 3].reshape(6 * kb, 2 * LANES)
            r = jnp.dot(pmat, vx, preferred_element_type=F32)
            for p in range(2):
                num = r[kb * p:kb * (p + 1), :LANES]
                den = r[kb * p:kb * (p + 1), LANES:] + sink_terms[p]
                o_scr[r0:r0 + kb, c0 + LANES * p:c0 + LANES * (p + 1)] = num / den

    y = (o_scr[...] * gate_ref[...].astype(F32)).astype(BF16)
    out_ref[...] = x_ref[...] + jnp.dot(y, wo_ref[...], preferred_element_type=F32)


def _attn_core(sink, q, kv, gate, x, wo, seq):
    n = x.shape[0]
    tq = min(ATTN_TILE, seq)
    n_t = seq // tq
    nblk = tq // KEY_BLOCK
    last_blk = n // KEY_BLOCK - 1
    row = lambda b, t: (b * n_t + t, 0)
    const = lambda b, t: (0, 0)
    prev = lambda b, t: (jnp.maximum((b * n_t + t) * nblk - 1, 0), 0)
    nxt = lambda b, t: (jnp.minimum((b * n_t + t + 1) * nblk, last_blk), 0)
    return pl.pallas_call(
        _attn_core_kernel,
        grid=(n // seq, n_t),
        in_specs=[pl.BlockSpec(memory_space=pltpu.SMEM),
                  pl.BlockSpec((tq, ATTN_WIDTH), row), pl.BlockSpec((tq, 2 * ATTN_KV_WIDTH), row),
                  pl.BlockSpec((KEY_BLOCK, 2 * ATTN_KV_WIDTH), prev), pl.BlockSpec((KEY_BLOCK, 2 * ATTN_KV_WIDTH), nxt),
                  pl.BlockSpec((tq, ATTN_WIDTH), row), pl.BlockSpec((tq, D_MODEL), row),
                  pl.BlockSpec((ATTN_WIDTH, D_MODEL), const)],
        out_specs=pl.BlockSpec((tq, D_MODEL), row),
        out_shape=jax.ShapeDtypeStruct((n, D_MODEL), F32),
        scratch_shapes=[pltpu.VMEM((ATTN_KV_HEADS, nblk + 2, 2, KEY_BLOCK, LANES), BF16),
                        pltpu.VMEM((ATTN_KV_HEADS, nblk + 2, 2, KEY_BLOCK, 2 * LANES), BF16),
                        pltpu.VMEM((tq, ATTN_WIDTH), F32)],
        compiler_params=_params(2),
        name="attn_core",
    )(sink, q, kv, kv, kv, gate, x, wo)


ROW_SLABS = 6


def _log_sigmoid(x):
    return jnp.minimum(x, 0.0) - jnp.log1p(jnp.exp(-jnp.abs(x)))


def _segment_scan(x, op, fill, pos_in_chunk, reverse):
    width = x.shape[1]
    d = 1
    while d < CHUNK:
        if reverse:
            shifted = pltpu.roll(x, width - d, 1)
            valid = pos_in_chunk < CHUNK - d
        else:
            shifted = pltpu.roll(x, d, 1)
            valid = pos_in_chunk >= d
        x = op(x, jnp.where(valid, shifted, fill))
        d *= 2
    return x


def _mlstm_in_kernel(x_ref, g_ref, w_ref, wg_ref, bias_ref, q_ref, kt_ref, v_ref, og_ref, colg_ref, rowg_ref, xn_scr):
    tm = x_ref.shape[0]
    xn_scr[...] = _normed_bf16(x_ref, g_ref)

    def proj(col, width):
        return jnp.dot(xn_scr[...], w_ref[:, col:col + width], preferred_element_type=F32)

    gates = jnp.dot(xn_scr[...], wg_ref[...], preferred_element_type=F32) + bias_ref[...]
    gt = gates.T
    s = GATE_SLAB
    ig_f, fg_f, ig_b, fg_b = gt[0:s], gt[s:2 * s], gt[2 * s:3 * s], gt[3 * s:4 * s]
    pos = lax.broadcasted_iota(jnp.int32, (s, tm), 1) % CHUNK
    b_f = _segment_scan(_log_sigmoid(fg_f) * LOG2E, jnp.add, 0.0, pos, False)
    u_f = ig_f * LOG2E - b_f
    cu_f = _segment_scan(u_f, jnp.maximum, -jnp.inf, pos, False)
    b_b = _segment_scan(_log_sigmoid(fg_b) * LOG2E, jnp.add, 0.0, pos, True)
    u_b = ig_b * LOG2E - b_b
    cu_b = _segment_scan(u_b, jnp.maximum, -jnp.inf, pos, True)
    pad = jnp.zeros((LANES - 4 * s, tm), F32)
    colg_ref[...] = jnp.concatenate([b_f, cu_f, b_b, cu_b, pad], axis=0).T
    rowg_ref[...] = jnp.concatenate([u_f, u_b, b_f, cu_f, b_b, cu_b], axis=0)

    for c in range(MLSTM_QK_WIDTH // 256):
        q_ref[:, 256 * c:256 * (c + 1)] = proj(256 * c, 256).astype(BF16)
    kscale = MLSTM_QK_DIM ** -0.5
    for c in range(MLSTM_QK_WIDTH // 256):
        kt_ref[256 * c:256 * (c + 1), :] = (proj(MLSTM_QK_WIDTH + 256 * c, 256) * kscale).T.astype(BF16)
    v0 = 2 * MLSTM_QK_WIDTH
    for c in range(MLSTM_WIDTH // 256):
        v_ref[:, 256 * c:256 * (c + 1)] = proj(v0 + 256 * c, 256).astype(BF16)
    o0 = v0 + MLSTM_WIDTH
    z0 = o0 + MLSTM_WIDTH
    for c in range(MLSTM_WIDTH // 256):
        o = proj(o0 + 256 * c, 256)
        z = proj(z0 + 256 * c, 256)
        og_ref[:, 256 * c:256 * (c + 1)] = (jax.nn.sigmoid(o) * (z * jax.nn.sigmoid(z))).astype(BF16)


def _mlstm_in(x, g, w, wg, bias, seq):
    n = x.shape[0]
    tm = min(PROJ_TILE, seq)
    row = lambda i: (i, 0)
    const = lambda i: (0, 0)
    col = lambda i: (0, i)
    return pl.pallas_call(
        _mlstm_in_kernel,
        grid=(n // tm,),
        in_specs=[pl.BlockSpec((tm, D_MODEL), row), pl.BlockSpec((1, D_MODEL), const),
                  pl.BlockSpec(w.shape, const), pl.BlockSpec(wg.shape, const), pl.BlockSpec((1, LANES), const)],
        out_specs=[pl.BlockSpec((tm, MLSTM_QK_WIDTH), row), pl.BlockSpec((MLSTM_QK_WIDTH, tm), col),
                   pl.BlockSpec((tm, MLSTM_WIDTH), row), pl.BlockSpec((tm, MLSTM_WIDTH), row),
                   pl.BlockSpec((tm, LANES), row), pl.BlockSpec((ROW_SLABS * GATE_SLAB, tm), col)],
        out_shape=[jax.ShapeDtypeStruct((n, MLSTM_QK_WIDTH), BF16), jax.ShapeDtypeStruct((MLSTM_QK_WIDTH, n), BF16),
                   jax.ShapeDtypeStruct((n, MLSTM_WIDTH), BF16), jax.ShapeDtypeStruct((n, MLSTM_WIDTH), BF16),
                   jax.ShapeDtypeStruct((n, LANES), F32), jax.ShapeDtypeStruct((ROW_SLABS * GATE_SLAB, n), F32)],
        scratch_shapes=[pltpu.VMEM((tm, D_MODEL), BF16)],
        compiler_params=_params(1),
        name="mlstm_in",
    )(x, g, w, wg, bias)


def _chunk_stats(rowg):
    s, nh = GATE_SLAB, MLSTM_HEADS
    last = rowg[:, CHUNK - 1::CHUNK]
    first = rowg[:, 0::CHUNK]
    stats = jnp.concatenate([last[2 * s:2 * s + nh], last[3 * s:3 * s + nh],
                             first[4 * s:4 * s + nh], first[5 * s:5 * s + nh]], axis=0)
    return stats.T.reshape(-1)


STATS_PER_CHUNK = 4 * MLSTM_HEADS


def _mlstm_sweep_tile(cstat_ref, chunk0, q_ref, kt_ref, v_ref, colg_ref, rowg_ref, c_scr, n_scr, m_scr, h_scr, vc_scr,
                      reverse, chunk_done=None):
    assert CHUNK == LANES
    tt = q_ref.shape[0]
    nch = tt // CHUNK
    L = CHUNK
    ti = lax.broadcasted_iota(jnp.int32, (L, L), 0)
    si = lax.broadcasted_iota(jnp.int32, (L, L), 1)
    causal = (si >= ti) if reverse else (si <= ti)
    col0 = 2 * GATE_SLAB if reverse else 0
    urow0 = GATE_SLAB if reverse else 0
    stat0 = 2 * MLSTM_HEADS if reverse else 0
    half = MLSTM_V_DIM // 2
    for slot in range(nch * MLSTM_HEADS):
        vc_scr[slot, 0:L, MLSTM_V_DIM:] = jnp.ones((L, LANES), BF16)

    chunks = range(nch - 1, -1, -1) if reverse else range(nch)

    m_prev, m_ref = {}, {}
    for h in range(MLSTM_HEADS):
        m = m_scr[h]
        for c in chunks:
            sbase = (chunk0 + c) * STATS_PER_CHUNK + stat0
            m_prev[c, h] = m
            m_ref[c, h] = jnp.maximum(m, cstat_ref[sbase + MLSTM_HEADS + h])
            m = m_ref[c, h] + cstat_ref[sbase + h]
        m_scr[h] = m

    lhs, floor, dr = {}, {}, {}
    for c in chunks:
        r0 = L * c
        cg = colg_ref[r0:r0 + L, :]
        for h in range(MLSTM_HEADS):
            slot = c * MLSTM_HEADS + h
            qh = q_ref[r0:r0 + L, MLSTM_QK_DIM * h:MLSTM_QK_DIM * (h + 1)]
            kth = kt_ref[MLSTM_QK_DIM * h:MLSTM_QK_DIM * (h + 1), r0:r0 + L]
            b_rep = jnp.broadcast_to(cg[:, col0 + h:col0 + h + 1], (L, LANES))
            cu_rep = jnp.broadcast_to(cg[:, col0 + GATE_SLAB + h:col0 + GATE_SLAB + h + 1], (L, LANES))
            u_r = rowg_ref[urow0 + h:urow0 + h + 1, r0:r0 + L]
            mx = jnp.maximum(cu_rep, m_prev[c, h])
            e = jnp.exp2(jnp.where(causal, u_r - mx, NEG))
            s = jnp.dot(qh, kth, preferred_element_type=F32)
            sc = jnp.exp2(m_prev[c, h] - mx)
            lhs[c, h] = jnp.concatenate([(e * s).astype(BF16), (qh.astype(F32) * sc).astype(BF16)], axis=1)
            floor[c, h] = jnp.exp2(-(b_rep + mx))
            vc_scr[slot, 0:L, 0:MLSTM_V_DIM] = v_ref[r0:r0 + L, MLSTM_V_DIM * h:MLSTM_V_DIM * (h + 1)]
    for c in chunks:
        r0 = L * c
        for h in range(MLSTM_HEADS):
            slot = c * MLSTM_HEADS + h
            kth = kt_ref[MLSTM_QK_DIM * h:MLSTM_QK_DIM * (h + 1), r0:r0 + L]
            u_r = rowg_ref[urow0 + h:urow0 + h + 1, r0:r0 + L]
            kwt = (kth.astype(F32) * jnp.exp2(u_r - m_ref[c, h])).astype(BF16)
            dr[c, h] = jnp.dot(kwt, vc_scr[slot, 0:L, :], preferred_element_type=F32)

    pending = None
    for c in chunks:
        r0 = L * c
        for h in range(MLSTM_HEADS):
            slot = c * MLSTM_HEADS + h
            vc_scr[slot, L:, 0:MLSTM_V_DIM] = c_scr[h].astype(BF16)
            vc_scr[slot, L:, MLSTM_V_DIM:] = n_scr[h].astype(BF16)
            r = jnp.dot(lhs[c, h], vc_scr[slot], preferred_element_type=F32)
            inv = 1.0 / jnp.maximum(jnp.abs(r[:, MLSTM_V_DIM:]), floor[c, h])
            c0 = MLSTM_V_DIM * h
            h_scr[r0:r0 + L, c0:c0 + half] = r[:, :half] * inv
            h_scr[r0:r0 + L, c0 + half:c0 + 2 * half] = r[:, half:2 * half] * inv
            sp = jnp.exp2(jnp.full((1, MLSTM_V_DIM), m_prev[c, h] - m_ref[c, h], F32))
            c_scr[h] = sp * c_scr[h] + dr[c, h][:, :MLSTM_V_DIM]
            n_scr[h] = sp[:, :LANES] * n_scr[h] + dr[c, h][:, MLSTM_V_DIM:]
        if chunk_done is not None:
            if pending is not None:
                chunk_done(pending)
            pending = r0
    if pending is not None:
        chunk_done(pending)


def _reset_state(c_scr, n_scr, m_scr):
    c_scr[...] = jnp.zeros(c_scr.shape, F32)
    n_scr[...] = jnp.zeros(n_scr.shape, F32)
    for h in range(MLSTM_HEADS):
        m_scr[h] = NEG_INIT


def _mlstm_bwd_kernel(cstat_ref, q_ref, kt_ref, v_ref, colg_ref, rowg_ref, hb_ref, c_scr, n_scr, m_scr, h_scr, vc_scr):
    b, t, n_t = pl.program_id(0), pl.program_id(1), pl.num_programs(1)

    @pl.when(t == 0)
    def _():
        _reset_state(c_scr, n_scr, m_scr)

    chunk0 = (b * n_t + n_t - 1 - t) * (q_ref.shape[0] // CHUNK)
    _mlstm_sweep_tile(cstat_ref, chunk0, q_ref, kt_ref, v_ref, colg_ref, rowg_ref, c_scr, n_scr, m_scr, h_scr,
                      vc_scr, reverse=True)
    hb_ref[...] = h_scr[...].astype(BF16)


def _mlstm_fwd_kernel(cstat_ref, q_ref, kt_ref, v_ref, colg_ref, rowg_ref, hb_ref, og_ref, hn_ref, x_ref, wo_ref,
                      fg_ref, out_ref, c_scr, n_scr, m_scr, h_scr, vc_scr, *, final_norm):
    b, t, n_t = pl.program_id(0), pl.program_id(1), pl.num_programs(1)

    @pl.when(t == 0)
    def _():
        _reset_state(c_scr, n_scr, m_scr)

    chunk0 = (b * n_t + t) * (q_ref.shape[0] // CHUNK)
    def finish_rows(r0):
        rows = slice(r0, r0 + CHUNK)
        for h in range(MLSTM_HEADS):
            cs = slice(MLSTM_V_DIM * h, MLSTM_V_DIM * (h + 1))
            hh = h_scr[rows, cs] + hb_ref[rows, cs].astype(F32)
            ms = jnp.mean(hh * hh, axis=-1, keepdims=True)
            hh = hh * lax.rsqrt(ms + EPS) * hn_ref[:, cs]
            h_scr[rows, cs] = hh * og_ref[rows, cs].astype(F32)
        y = x_ref[rows, :] + jnp.dot(h_scr[rows, :].astype(BF16), wo_ref[...], preferred_element_type=F32)
        if final_norm:
            ms = jnp.mean(y * y, axis=-1, keepdims=True)
            y = y * lax.rsqrt(ms + EPS) * fg_ref[...]
        out_ref[rows, :] = y

    _mlstm_sweep_tile(cstat_ref, chunk0, q_ref, kt_ref, v_ref, colg_ref, rowg_ref, c_scr, n_scr, m_scr, h_scr,
                      vc_scr, reverse=False, chunk_done=finish_rows)


def _mlstm_state_scratch(tt):
    return [pltpu.VMEM((MLSTM_HEADS, MLSTM_QK_DIM, MLSTM_V_DIM), F32),
            pltpu.VMEM((MLSTM_HEADS, MLSTM_QK_DIM, LANES), F32),
            pltpu.SMEM((MLSTM_HEADS,), F32),
            pltpu.VMEM((tt, MLSTM_WIDTH), F32),
            pltpu.VMEM((tt // CHUNK * MLSTM_HEADS, CHUNK + MLSTM_QK_DIM, MLSTM_V_DIM + LANES), BF16)]


def _mlstm_core(q, kt, v, colg, rowg, og, hn, x, wo, fg, seq, final_norm):
    n = x.shape[0]
    tt = min(MLSTM_TILE, seq)
    n_t = seq // tt
    cstat = _chunk_stats(rowg)
    const = lambda b, t, cs: (0, 0)
    fwd_row = lambda b, t, cs: (b * n_t + t, 0)
    fwd_col = lambda b, t, cs: (0, b * n_t + t)
    bwd_row = lambda b, t, cs: (b * n_t + n_t - 1 - t, 0)
    bwd_col = lambda b, t, cs: (0, b * n_t + n_t - 1 - t)

    def seq_specs(row, col):
        return [pl.BlockSpec((tt, MLSTM_QK_WIDTH), row), pl.BlockSpec((MLSTM_QK_WIDTH, tt), col),
                pl.BlockSpec((tt, MLSTM_WIDTH), row), pl.BlockSpec((tt, LANES), row),
                pl.BlockSpec((ROW_SLABS * GATE_SLAB, tt), col)]

    hb = pl.pallas_call(
        _mlstm_bwd_kernel,
        grid_spec=pltpu.PrefetchScalarGridSpec(
            num_scalar_prefetch=1, grid=(n // seq, n_t),
            in_specs=seq_specs(bwd_row, bwd_col),
            out_specs=pl.BlockSpec((tt, MLSTM_WIDTH), bwd_row),
            scratch_shapes=_mlstm_state_scratch(tt)),
        out_shape=jax.ShapeDtypeStruct((n, MLSTM_WIDTH), BF16),
        compiler_params=_params(2),
        name="mlstm_bwd",
    )(cstat, q, kt, v, colg, rowg)

    return pl.pallas_call(
        functools.partial(_mlstm_fwd_kernel, final_norm=final_norm),
        grid_spec=pltpu.PrefetchScalarGridSpec(
            num_scalar_prefetch=1, grid=(n // seq, n_t),
            in_specs=seq_specs(fwd_row, fwd_col) + [
                pl.BlockSpec((tt, MLSTM_WIDTH), fwd_row), pl.BlockSpec((tt, MLSTM_WIDTH), fwd_row),
                pl.BlockSpec((1, MLSTM_WIDTH), const), pl.BlockSpec((tt, D_MODEL), fwd_row),
                pl.BlockSpec((MLSTM_WIDTH, D_MODEL), const), pl.BlockSpec((1, D_MODEL), const)],
            out_specs=pl.BlockSpec((tt, D_MODEL), fwd_row),
            scratch_shapes=_mlstm_state_scratch(tt)),
        out_shape=jax.ShapeDtypeStruct((n, D_MODEL), F32),
        compiler_params=_params(2),
        name="mlstm_fwd",
    )(cstat, q, kt, v, colg, rowg, hb, og, hn, x, wo, fg)


def _rope_tables(seq):
    half = ATTN_HEAD_DIM // 2
    inv = jnp.exp(-math.log(ROPE_THETA) * jnp.arange(half, dtype=F32) / half)
    ang = jnp.arange(seq).astype(F32)[:, None] * inv[None, :]
    cos = jnp.cos(ang)
    sin = jnp.sin(ang)
    cos_t = jnp.tile(cos, (1, LANES // half))
    sin_t = jnp.tile(jnp.concatenate([-sin, sin], axis=1), (1, LANES // ATTN_HEAD_DIM))
    return cos_t, sin_t


def _gate_slabs(a):
    lead = a.shape[:-1]
    a = a.reshape(lead + (4, MLSTM_HEADS))
    a = jnp.pad(a, [(0, 0)] * len(lead) + [(0, 0), (0, GATE_SLAB - MLSTM_HEADS)])
    a = a.reshape(lead + (4 * GATE_SLAB,))
    return jnp.pad(a, [(0, 0)] * len(lead) + [(0, LANES - 4 * GATE_SLAB)])


def _trunk(x, p):
    bsz, seq, _ = x.shape
    xf = x.reshape(bsz * seq, D_MODEL)
    cos_t, sin_t = _rope_tables(seq)
    for i in range(4):
        j = i // 2
        g = p["norm_g"][i][None, :]
        if i % 2 == 0:
            q, kv, gate = _attn_in(xf, g, p["attn_w_in"][j], cos_t, sin_t, seq)
            xf = _attn_core(p["attn_sink"][j], q, kv, gate, xf, p["attn_w_out"][j], seq)
        else:
            q, kt, v, og, colg, rowg = _mlstm_in(xf, g, p["mlstm_w_main"][j], p["mlstm_w_gate"][j],
                                                 p["mlstm_gate_bias"][j], seq)
            xf = _mlstm_core(q, kt, v, colg, rowg, og, p["mlstm_head_norm"][j][None, :], xf, p["mlstm_w_out"][j],
                             p["final_norm_g"][None, :], seq, final_norm=(i == 3))
    return xf.reshape(bsz, seq, D_MODEL)


def kernel(x_prompt, x_sample, norm_g, attn_w_in, attn_sink, attn_w_out, mlstm_w_in, mlstm_gate_bias, mlstm_head_norm, mlstm_w_out, final_norm_g):
    p = {
        "norm_g": norm_g.astype(F32),
        "attn_w_in": attn_w_in.astype(BF16),
        "attn_sink": attn_sink.astype(F32),
        "attn_w_out": attn_w_out.astype(BF16),
        "mlstm_w_main": mlstm_w_in[:, :, :MLSTM_MAIN_IN].astype(BF16),
        "mlstm_w_gate": _gate_slabs(mlstm_w_in[:, :, MLSTM_MAIN_IN:]).astype(BF16),
        "mlstm_gate_bias": _gate_slabs(mlstm_gate_bias.astype(F32))[:, None, :],
        "mlstm_head_norm": mlstm_head_norm.astype(F32),
        "mlstm_w_out": mlstm_w_out.astype(BF16),
        "final_norm_g": final_norm_g.astype(F32),
    }
    return _trunk(x_prompt, p), _trunk(x_sample, p)
```

```python
import functools
import math

import jax
import jax.numpy as jnp
from jax import lax
from jax.experimental import pallas as pl
from jax.experimental.pallas import tpu as pltpu

F32 = jnp.float32
BF16 = jnp.bfloat16

D_MODEL = 1024
EPS = 1e-6
NEG = -1e30
LOG2E = math.log2(math.e)

ATTN_HEADS = 16
ATTN_KV_HEADS = 4
ATTN_HEAD_DIM = 64
ATTN_WIDTH = ATTN_HEADS * ATTN_HEAD_DIM
ATTN_KV_WIDTH = ATTN_KV_HEADS * ATTN_HEAD_DIM
WINDOW = 128
ROPE_THETA = 10000.0
KEY_BLOCK = 128

MLSTM_HEADS = 4
MLSTM_V_DIM = 256
MLSTM_QK_DIM = 128
MLSTM_WIDTH = MLSTM_HEADS * MLSTM_V_DIM
MLSTM_QK_WIDTH = MLSTM_HEADS * MLSTM_QK_DIM
MLSTM_MAIN_IN = 2 * MLSTM_QK_WIDTH + 3 * MLSTM_WIDTH
NEG_INIT = -1e30
CHUNK = 128
GATE_SLAB = 8

LANES = 128
VMEM_LIMIT = 56 * 1024 * 1024

PROJ_TILE = 512
ATTN_TILE = 1024
MLSTM_TILE = 1024


def _params(n_axes):
    return pltpu.CompilerParams(dimension_semantics=("arbitrary",) * n_axes, vmem_limit_bytes=VMEM_LIMIT)


def _normed_bf16(x_ref, g_ref):
    x = x_ref[...]
    ms = jnp.mean(x * x, axis=-1, keepdims=True)
    return (x * lax.rsqrt(ms + EPS) * g_ref[...]).astype(BF16)


def _attn_in_kernel(x_ref, g_ref, w_ref, cos_ref, sin_ref, q_ref, kv_ref, gate_ref, xn_scr):
    tm = x_ref.shape[0]
    xn_scr[...] = _normed_bf16(x_ref, g_ref)
    cos = cos_ref[...]
    sin = sin_ref[...]
    lane = lax.broadcasted_iota(jnp.int32, (tm, LANES), 1)
    first_half = (lane % ATTN_HEAD_DIM) < (ATTN_HEAD_DIM // 2)

    def rope(a):
        partner = jnp.where(first_half, pltpu.roll(a, LANES - 32, 1), pltpu.roll(a, 32, 1))
        return a * cos + partner * sin

    def proj(col, width):
        return jnp.dot(xn_scr[...], w_ref[:, col:col + width], preferred_element_type=F32)

    scale = ATTN_HEAD_DIM ** -0.5 * LOG2E
    for c in range(ATTN_WIDTH // 256):
        acc = proj(256 * c, 256)
        for j in range(2):
            q_ref[:, 256 * c + LANES * j:256 * c + LANES * (j + 1)] = (
                rope(acc[:, LANES * j:LANES * (j + 1)]) * scale).astype(BF16)
    acc = proj(ATTN_WIDTH, 2 * ATTN_KV_WIDTH)
    for j in range(ATTN_KV_WIDTH // LANES):
        kv_ref[:, LANES * j:LANES * (j + 1)] = rope(acc[:, LANES * j:LANES * (j + 1)]).astype(BF16)
    kv_ref[:, ATTN_KV_WIDTH:] = acc[:, ATTN_KV_WIDTH:].astype(BF16)
    z0 = ATTN_WIDTH + 2 * ATTN_KV_WIDTH
    for c in range(ATTN_WIDTH // 256):
        z = proj(z0 + 256 * c, 256)
        gate_ref[:, 256 * c:256 * (c + 1)] = (z * jax.nn.sigmoid(z)).astype(BF16)


def _attn_in(x, g, w, cos_t, sin_t, seq):
    n = x.shape[0]
    tm = min(PROJ_TILE, seq)
    tiles_per_seq = seq // tm
    row = lambda i: (i, 0)
    const = lambda i: (0, 0)
    pos = lambda i: (i % tiles_per_seq, 0)
    return pl.pallas_call(
        _attn_in_kernel,
        grid=(n // tm,),
        in_specs=[pl.BlockSpec((tm, D_MODEL), row), pl.BlockSpec((1, D_MODEL), const),
                  pl.BlockSpec(w.shape, const), pl.BlockSpec((tm, LANES), pos), pl.BlockSpec((tm, LANES), pos)],
        out_specs=[pl.BlockSpec((tm, ATTN_WIDTH), row), pl.BlockSpec((tm, 2 * ATTN_KV_WIDTH), row),
                   pl.BlockSpec((tm, ATTN_WIDTH), row)],
        out_shape=[jax.ShapeDtypeStruct((n, ATTN_WIDTH), BF16), jax.ShapeDtypeStruct((n, 2 * ATTN_KV_WIDTH), BF16),
                   jax.ShapeDtypeStruct((n, ATTN_WIDTH), BF16)],
        scratch_shapes=[pltpu.VMEM((tm, D_MODEL), BF16)],
        compiler_params=_params(1),
        name="attn_in",
    )(x, g, w, cos_t, sin_t)


def _attn_core_kernel(sink_ref, q_ref, kv_ref, kvp_ref, kvn_ref, gate_ref, x_ref, wo_ref, out_ref,
                      kx_scr, vx_scr, o_scr):
    t = pl.program_id(1)
    n_t = pl.num_programs(1)
    tq = q_ref.shape[0]
    nblk = tq // KEY_BLOCK
    kb = KEY_BLOCK
    half = ATTN_HEAD_DIM

    lane = lax.broadcasted_iota(jnp.int32, (kb, LANES), 1)
    low = lane < half
    ones_lo = jnp.where(low, 1.0, 0.0).astype(BF16)
    ones_hi = jnp.where(low, 0.0, 1.0).astype(BF16)

    def expand(blk_ref, r0, jb):
        for vi in range(ATTN_KV_WIDTH // LANES):
            kcol = blk_ref[r0:r0 + kb, LANES * vi:LANES * (vi + 1)].astype(F32)
            vcol = blk_ref[r0:r0 + kb, ATTN_KV_WIDTH + LANES * vi:ATTN_KV_WIDTH + LANES * (vi + 1)].astype(F32)
            kswap = pltpu.roll(kcol, half, 1)
            vswap = pltpu.roll(vcol, half, 1)
            for hf in range(2):
                g = 2 * vi + hf
                k_src_lo, k_src_hi = (kcol, kswap) if hf == 0 else (kswap, kcol)
                v_src_lo, v_src_hi = (vcol, vswap) if hf == 0 else (vswap, vcol)
                kx_scr[g, jb, 0] = jnp.where(low, k_src_lo, 0.0).astype(BF16)
                kx_scr[g, jb, 1] = jnp.where(low, 0.0, k_src_hi).astype(BF16)
                vx_scr[g, jb, 0, :, :LANES] = jnp.where(low, v_src_lo, 0.0).astype(BF16)
                vx_scr[g, jb, 0, :, LANES:] = ones_lo
                vx_scr[g, jb, 1, :, :LANES] = jnp.where(low, 0.0, v_src_hi).astype(BF16)
                vx_scr[g, jb, 1, :, LANES:] = ones_hi

    expand(kvp_ref, 0, 0)
    for jb in range(nblk):
        expand(kv_ref, kb * jb, jb + 1)
    expand(kvn_ref, 0, nblk + 1)

    qi = lax.broadcasted_iota(jnp.int32, (kb, kb), 0)
    ki = lax.broadcasted_iota(jnp.int32, (kb, kb), 1)
    band_prev = jnp.where(ki >= qi, 0.0, NEG).astype(F32)
    band_next = jnp.where(ki <= qi, 0.0, NEG).astype(F32)

    for blk in range(nblk):
        r0 = kb * blk
        bias_prev = jnp.where(t == 0, NEG, band_prev) if blk == 0 else band_prev
        bias_next = jnp.where(t == n_t - 1, NEG, band_next) if blk == nblk - 1 else band_next
        for g in range(ATTN_KV_HEADS):
            c0 = 2 * LANES * g
            q2 = jnp.concatenate([q_ref[r0:r0 + kb, c0:c0 + LANES],
                                  q_ref[r0:r0 + kb, c0 + LANES:c0 + 2 * LANES]], axis=0)
            kx = kx_scr[g, blk:blk + 3].reshape(6 * kb, LANES)
            s = lax.dot_general(q2, kx, (((1,), (1,)), ((), ())), preferred_element_type=F32)
            p_rows = []
            sink_terms = []
            for p in range(2):
                cols = [None] * 6
                maxes = []
                sinks = []
                for ab in range(2):
                    head = 4 * g + 2 * p + ab
                    sink = sink_ref[head] * LOG2E
                    sp = s[kb * p:kb * (p + 1), kb * ab:kb * (ab + 1)] + bias_prev
                    so = s[kb * p:kb * (p + 1), kb * (2 + ab):kb * (3 + ab)]
                    sn = s[kb * p:kb * (p + 1), kb * (4 + ab):kb * (5 + ab)] + bias_next
                    m = jnp.max(jnp.maximum(jnp.maximum(sp, so), sn), axis=1, keepdims=True)
                    m = jnp.maximum(m, sink)
                    cols[ab] = jnp.exp2(sp - m).astype(BF16)
                    cols[2 + ab] = jnp.exp2(so - m).astype(BF16)
                    cols[4 + ab] = jnp.exp2(sn - m).astype(BF16)
                    maxes.append(m)
                    sinks.append(sink)
                p_rows.append(jnp.concatenate(cols, axis=1))
                sink_terms.append(jnp.exp2(jnp.where(low, sinks[0], sinks[1]) - jnp.where(low, maxes[0], maxes[1])))
            pmat = jnp.concatenate(p_rows, axis=0)
            vx = vx_scr[g, blk:blk + 3].reshape(6 * kb, 2 * LANES)
            r = jnp.dot(pmat, vx, preferred_element_type=F32)
            for p in range(2):
                num = r[kb * p:kb * (p + 1), :LANES]
                den = r[kb * p:kb * (p + 1), LANES:] + sink_terms[p]
                o_scr[r0:r0 + kb, c0 + LANES * p:c0 + LANES * (p + 1)] = num / den

    y = (o_scr[...] * gate_ref[...].astype(F32)).astype(BF16)
    out_ref[...] = x_ref[...] + jnp.dot(y, wo_ref[...], preferred_element_type=F32)


def _attn_core(sink, q, kv, gate, x, wo, seq):
    n = x.shape[0]
    tq = min(ATTN_TILE, seq)
    n_t = seq // tq
    nblk = tq // KEY_BLOCK
    last_blk = n // KEY_BLOCK - 1
    row = lambda b, t: (b * n_t + t, 0)
    const = lambda b, t: (0, 0)
    prev = lambda b, t: (jnp.maximum((b * n_t + t) * nblk - 1, 0), 0)
    nxt = lambda b, t: (jnp.minimum((b * n_t + t + 1) * nblk, last_blk), 0)
    return pl.pallas_call(
        _attn_core_kernel,
        grid=(n // seq, n_t),
        in_specs=[pl.BlockSpec(memory_space=pltpu.SMEM),
                  pl.BlockSpec((tq, ATTN_WIDTH), row), pl.BlockSpec((tq, 2 * ATTN_KV_WIDTH), row),
                  pl.BlockSpec((KEY_BLOCK, 2 * ATTN_KV_WIDTH), prev), pl.BlockSpec((KEY_BLOCK, 2 * ATTN_KV_WIDTH), nxt),
                  pl.BlockSpec((tq, ATTN_WIDTH), row), pl.BlockSpec((tq, D_MODEL), row),
                  pl.BlockSpec((ATTN_WIDTH, D_MODEL), const)],
        out_specs=pl.BlockSpec((tq, D_MODEL), row),
        out_shape=jax.ShapeDtypeStruct((n, D_MODEL), F32),
        scratch_shapes=[pltpu.VMEM((ATTN_KV_HEADS, nblk + 2, 2, KEY_BLOCK, LANES), BF16),
                        pltpu.VMEM((ATTN_KV_HEADS, nblk + 2, 2, KEY_BLOCK, 2 * LANES), BF16),
                        pltpu.VMEM((tq, ATTN_WIDTH), F32)],
        compiler_params=_params(2),
        name="attn_core",
    )(sink, q, kv, kv, kv, gate, x, wo)


ROW_SLABS = 6


def _log_sigmoid(x):
    return jnp.minimum(x, 0.0) - jnp.log1p(jnp.exp(-jnp.abs(x)))


def _segment_scan(x, op, fill, pos_in_chunk, reverse):
    width = x.shape[1]
    d = 1
    while d < CHUNK:
        if reverse:
            shifted = pltpu.roll(x, width - d, 1)
            valid = pos_in_chunk < CHUNK - d
        else:
            shifted = pltpu.roll(x, d, 1)
            valid = pos_in_chunk >= d
        x = op(x, jnp.where(valid, shifted, fill))
        d *= 2
    return x


def _mlstm_project(x_ref, g_ref, w_ref, wg_ref, bias_ref, q_ref, kt_ref, v_ref, og_ref, colg_ref, rowg_ref, xn_scr):
    tm = x_ref.shape[0]
    xn_scr[...] = _normed_bf16(x_ref, g_ref)

    def proj(col, width):
        return jnp.dot(xn_scr[...], w_ref[:, col:col + width], preferred_element_type=F32)

    gates = jnp.dot(xn_scr[...], wg_ref[...], preferred_element_type=F32) + bias_ref[...]
    gt = gates.T
    s = GATE_SLAB
    ig_f, fg_f, ig_b, fg_b = gt[0:s], gt[s:2 * s], gt[2 * s:3 * s], gt[3 * s:4 * s]
    pos = lax.broadcasted_iota(jnp.int32, (s, tm), 1) % CHUNK
    b_f = _segment_scan(_log_sigmoid(fg_f) * LOG2E, jnp.add, 0.0, pos, False)
    u_f = ig_f * LOG2E - b_f
    cu_f = _segment_scan(u_f, jnp.maximum, -jnp.inf, pos, False)
    b_b = _segment_scan(_log_sigmoid(fg_b) * LOG2E, jnp.add, 0.0, pos, True)
    u_b = ig_b * LOG2E - b_b
    cu_b = _segment_scan(u_b, jnp.maximum, -jnp.inf, pos, True)
    pad = jnp.zeros((LANES - 4 * s, tm), F32)
    colg_ref[...] = jnp.concatenate([-b_f, cu_f, -b_b, cu_b, pad], axis=0).T
    rowg_ref[...] = jnp.concatenate([u_f, u_b, b_f, cu_f, b_b, cu_b], axis=0)

    for c in range(MLSTM_QK_WIDTH // 256):
        q_ref[:, 256 * c:256 * (c + 1)] = proj(256 * c, 256).astype(BF16)
    kscale = MLSTM_QK_DIM ** -0.5
    for c in range(MLSTM_QK_WIDTH // 256):
        kt_ref[256 * c:256 * (c + 1), :] = (proj(MLSTM_QK_WIDTH + 256 * c, 256) * kscale).T.astype(BF16)
    v0 = 2 * MLSTM_QK_WIDTH
    for c in range(MLSTM_WIDTH // 256):
        v_ref[:, 256 * c:256 * (c + 1)] = proj(v0 + 256 * c, 256).astype(BF16)
    o0 = v0 + MLSTM_WIDTH
    z0 = o0 + MLSTM_WIDTH

    def gate_chunk(c):
        o = proj(o0 + 256 * c, 256)
        z = proj(z0 + 256 * c, 256)
        og_ref[:, 256 * c:256 * (c + 1)] = (jax.nn.sigmoid(o) * (z * jax.nn.sigmoid(z))).astype(BF16)

    return [functools.partial(gate_chunk, c) for c in range(MLSTM_WIDTH // 256)]


def _mlstm_sweep_tile(q_ref, kt_ref, v_ref, colg_ref, rowg_ref, c_scr, n_scr, m_scr, h_scr, vc_scr,
                      reverse, chunk_done=None, fillers=()):
    assert CHUNK == LANES
    tt = q_ref.shape[0]
    nch = tt // CHUNK
    L = CHUNK
    ti = lax.broadcasted_iota(jnp.int32, (L, L), 0)
    si = lax.broadcasted_iota(jnp.int32, (L, L), 1)
    causal = (si >= ti) if reverse else (si <= ti)
    col0 = 2 * GATE_SLAB if reverse else 0
    urow0 = GATE_SLAB if reverse else 0
    half = MLSTM_V_DIM // 2
    for slot in range(nch * MLSTM_HEADS):
        vc_scr[slot, 0:L, MLSTM_V_DIM:] = jnp.ones((L, LANES), BF16)

    chunks = range(nch - 1, -1, -1) if reverse else range(nch)

    brow0 = (4 if reverse else 2) * GATE_SLAB
    m_prev, m_ref = {}, {}
    m = m_scr[...]
    for c in chunks:
        end = L * c if reverse else L * c + L - 1
        b_end = jnp.broadcast_to(rowg_ref[brow0:brow0 + GATE_SLAB, end:end + 1], (GATE_SLAB, LANES))
        cu_end = jnp.broadcast_to(rowg_ref[brow0 + GATE_SLAB:brow0 + 2 * GATE_SLAB, end:end + 1], (GATE_SLAB, LANES))
        m_new_minus_b = jnp.maximum(m, cu_end)
        for h in range(MLSTM_HEADS):
            m_prev[c, h] = m[h:h + 1, :]
            m_ref[c, h] = m_new_minus_b[h:h + 1, :]
        m = m_new_minus_b + b_end
    m_scr[...] = m

    fillers = list(fillers)
    lhs, floor, dr = {}, {}, {}
    for c in chunks:
        r0 = L * c
        cg = colg_ref[r0:r0 + L, :]
        if fillers:
            fillers.pop(0)()
        for h in range(MLSTM_HEADS):
            slot = c * MLSTM_HEADS + h
            qh = q_ref[r0:r0 + L, MLSTM_QK_DIM * h:MLSTM_QK_DIM * (h + 1)]
            kth = kt_ref[MLSTM_QK_DIM * h:MLSTM_QK_DIM * (h + 1), r0:r0 + L]
            nb_rep = jnp.broadcast_to(cg[:, col0 + h:col0 + h + 1], (L, LANES))
            cu_rep = jnp.broadcast_to(cg[:, col0 + GATE_SLAB + h:col0 + GATE_SLAB + h + 1], (L, LANES))
            u_r = rowg_ref[urow0 + h:urow0 + h + 1, r0:r0 + L]
            mx = jnp.maximum(cu_rep, m_prev[c, h])
            e = jnp.exp2(jnp.where(causal, u_r - mx, NEG))
            s = jnp.dot(qh, kth, preferred_element_type=F32)
            sc = jnp.exp2(m_prev[c, h] - mx)
            lhs[c, h] = jnp.concatenate([(e * s).astype(BF16), qh * sc.astype(BF16)], axis=1)
            floor[c, h] = jnp.exp2(nb_rep - mx)
            vc_scr[slot, 0:L, 0:MLSTM_V_DIM] = v_ref[r0:r0 + L, MLSTM_V_DIM * h:MLSTM_V_DIM * (h + 1)]

    pending = None
    for c in chunks:
        r0 = L * c
        for h in range(MLSTM_HEADS):
            slot = c * MLSTM_HEADS + h
            kth = kt_ref[MLSTM_QK_DIM * h:MLSTM_QK_DIM * (h + 1), r0:r0 + L]
            u_r = rowg_ref[urow0 + h:urow0 + h + 1, r0:r0 + L]
            kwt = kth * jnp.exp2(u_r - m_ref[c, h]).astype(BF16)
            dr[c, h] = jnp.dot(kwt, vc_scr[slot, 0:L, :], preferred_element_type=F32)
        for h in range(MLSTM_HEADS):
            slot = c * MLSTM_HEADS + h
            vc_scr[slot, L:, 0:MLSTM_V_DIM] = c_scr[h].astype(BF16)
            vc_scr[slot, L:, MLSTM_V_DIM:] = n_scr[h].astype(BF16)
            r = jnp.dot(lhs[c, h], vc_scr[slot], preferred_element_type=F32)
            inv = 1.0 / jnp.maximum(jnp.abs(r[:, MLSTM_V_DIM:]), floor[c, h])
            c0 = MLSTM_V_DIM * h
            h_scr[r0:r0 + L, c0:c0 + half] = r[:, :half] * inv
            h_scr[r0:r0 + L, c0 + half:c0 + 2 * half] = r[:, half:2 * half] * inv
            sp = jnp.exp2(m_prev[c, h] - m_ref[c, h])
            c_scr[h] = jnp.concatenate([sp, sp], axis=1) * c_scr[h] + dr[c, h][:, :MLSTM_V_DIM]
            n_scr[h] = sp * n_scr[h] + dr[c, h][:, MLSTM_V_DIM:]
        if chunk_done is not None:
            if pending is not None:
                chunk_done(pending)
            pending = r0
    if pending is not None:
        chunk_done(pending)
    for filler in fillers:
        filler()


def _reset_state(c_scr, n_scr, m_scr):
    c_scr[...] = jnp.zeros(c_scr.shape, F32)
    n_scr[...] = jnp.zeros(n_scr.shape, F32)
    m_scr[...] = jnp.full(m_scr.shape, NEG_INIT, F32)


def _mlstm_in_bwd_kernel(x_ref, g_ref, w_ref, wg_ref, bias_ref, q_ref, kt_ref, v_ref, og_ref, colg_ref, rowg_ref,
                         hb_ref, xn_scr, c_scr, n_scr, m_scr, h_scr, vc_scr):
    @pl.when(pl.program_id(1) == 0)
    def _():
        _reset_state(c_scr, n_scr, m_scr)

    gate_chunks = _mlstm_project(x_ref, g_ref, w_ref, wg_ref, bias_ref, q_ref, kt_ref, v_ref, og_ref, colg_ref,
                                 rowg_ref, xn_scr)
    _mlstm_sweep_tile(q_ref, kt_ref, v_ref, colg_ref, rowg_ref, c_scr, n_scr, m_scr, h_scr, vc_scr, reverse=True,
                      fillers=gate_chunks)
    hb_ref[...] = h_scr[...].astype(BF16)


def _mlstm_fwd_kernel(q_ref, kt_ref, v_ref, colg_ref, rowg_ref, hb_ref, og_ref, hn_ref, x_ref, wo_ref,
                      fg_ref, out_ref, c_scr, n_scr, m_scr, h_scr, vc_scr, *, final_norm):
    @pl.when(pl.program_id(1) == 0)
    def _():
        _reset_state(c_scr, n_scr, m_scr)

    def finish_rows(r0):
        rows = slice(r0, r0 + CHUNK)
        for h in range(MLSTM_HEADS):
            cs = slice(MLSTM_V_DIM * h, MLSTM_V_DIM * (h + 1))
            hh = h_scr[rows, cs] + hb_ref[rows, cs].astype(F32)
            ms = jnp.mean(hh * hh, axis=-1, keepdims=True)
            hh = hh * lax.rsqrt(ms + EPS) * hn_ref[:, cs]
            h_scr[rows, cs] = hh * og_ref[rows, cs].astype(F32)
        y = x_ref[rows, :] + jnp.dot(h_scr[rows, :].astype(BF16), wo_ref[...], preferred_element_type=F32)
        if final_norm:
            ms = jnp.mean(y * y, axis=-1, keepdims=True)
            y = y * lax.rsqrt(ms + EPS) * fg_ref[...]
        out_ref[rows, :] = y

    _mlstm_sweep_tile(q_ref, kt_ref, v_ref, colg_ref, rowg_ref, c_scr, n_scr, m_scr, h_scr, vc_scr, reverse=False,
                      chunk_done=finish_rows)


def _mlstm_state_scratch(tt):
    return [pltpu.VMEM((MLSTM_HEADS, MLSTM_QK_DIM, MLSTM_V_DIM), F32),
            pltpu.VMEM((MLSTM_HEADS, MLSTM_QK_DIM, LANES), F32),
            pltpu.VMEM((GATE_SLAB, LANES), F32),
            pltpu.VMEM((tt, MLSTM_WIDTH), F32),
            pltpu.VMEM((tt // CHUNK * MLSTM_HEADS, CHUNK + MLSTM_QK_DIM, MLSTM_V_DIM + LANES), BF16)]


def _mlstm_in_bwd(x, g, w, wg, bias, seq):
    n = x.shape[0]
    tt = min(MLSTM_TILE, seq)
    n_t = seq // tt
    const = lambda b, t: (0, 0)
    row = lambda b, t: (b * n_t + n_t - 1 - t, 0)
    col = lambda b, t: (0, b * n_t + n_t - 1 - t)
    return pl.pallas_call(
        _mlstm_in_bwd_kernel,
        grid=(n // seq, n_t),
        in_specs=[pl.BlockSpec((tt, D_MODEL), row), pl.BlockSpec((1, D_MODEL), const),
                  pl.BlockSpec(w.shape, const, pipeline_mode=pl.Buffered(1)), pl.BlockSpec(wg.shape, const),
                  pl.BlockSpec((1, LANES), const)],
        out_specs=[pl.BlockSpec((tt, MLSTM_QK_WIDTH), row), pl.BlockSpec((MLSTM_QK_WIDTH, tt), col),
                   pl.BlockSpec((tt, MLSTM_WIDTH), row), pl.BlockSpec((tt, MLSTM_WIDTH), row),
                   pl.BlockSpec((tt, LANES), row), pl.BlockSpec((ROW_SLABS * GATE_SLAB, tt), col),
                   pl.BlockSpec((tt, MLSTM_WIDTH), row)],
        out_shape=[jax.ShapeDtypeStruct((n, MLSTM_QK_WIDTH), BF16), jax.ShapeDtypeStruct((MLSTM_QK_WIDTH, n), BF16),
                   jax.ShapeDtypeStruct((n, MLSTM_WIDTH), BF16), jax.ShapeDtypeStruct((n, MLSTM_WIDTH), BF16),
                   jax.ShapeDtypeStruct((n, LANES), F32), jax.ShapeDtypeStruct((ROW_SLABS * GATE_SLAB, n), F32),
                   jax.ShapeDtypeStruct((n, MLSTM_WIDTH), BF16)],
        scratch_shapes=[pltpu.VMEM((tt, D_MODEL), BF16)] + _mlstm_state_scratch(tt),
        compiler_params=_params(2),
        name="mlstm_in_bwd",
    )(x, g, w, wg, bias)


def _mlstm_fwd(q, kt, v, colg, rowg, hb, og, hn, x, wo, fg, seq, final_norm):
    n = x.shape[0]
    tt = min(MLSTM_TILE, seq)
    n_t = seq // tt
    const = lambda b, t: (0, 0)
    row = lambda b, t: (b * n_t + t, 0)
    col = lambda b, t: (0, b * n_t + t)
    return pl.pallas_call(
        functools.partial(_mlstm_fwd_kernel, final_norm=final_norm),
        grid=(n // seq, n_t),
        in_specs=[pl.BlockSpec((tt, MLSTM_QK_WIDTH), row), pl.BlockSpec((MLSTM_QK_WIDTH, tt), col),
                  pl.BlockSpec((tt, MLSTM_WIDTH), row), pl.BlockSpec((tt, LANES), row),
                  pl.BlockSpec((ROW_SLABS * GATE_SLAB, tt), col),
                  pl.BlockSpec((tt, MLSTM_WIDTH), row), pl.BlockSpec((tt, MLSTM_WIDTH), row),
                  pl.BlockSpec((1, MLSTM_WIDTH), const), pl.BlockSpec((tt, D_MODEL), row),
                  pl.BlockSpec((MLSTM_WIDTH, D_MODEL), const), pl.BlockSpec((1, D_MODEL), const)],
        out_specs=pl.BlockSpec((tt, D_MODEL), row),
        out_shape=jax.ShapeDtypeStruct((n, D_MODEL), F32),
        scratch_shapes=_mlstm_state_scratch(tt),
        compiler_params=_params(2),
        name="mlstm_fwd",
    )(q, kt, v, colg, rowg, hb, og, hn, x, wo, fg)


def _rope_tables(seq):
    half = ATTN_HEAD_DIM // 2
    inv = jnp.exp(-math.log(ROPE_THETA) * jnp.arange(half, dtype=F32) / half)
    ang = jnp.arange(seq).astype(F32)[:, None] * inv[None, :]
    cos = jnp.cos(ang)
    sin = jnp.sin(ang)
    cos_t = jnp.tile(cos, (1, LANES // half))
    sin_t = jnp.tile(jnp.concatenate([-sin, sin], axis=1), (1, LANES // ATTN_HEAD_DIM))
    return cos_t, sin_t


def _gate_slabs(a):
    lead = a.shape[:-1]
    a = a.reshape(lead + (4, MLSTM_HEADS))
    a = jnp.pad(a, [(0, 0)] * len(lead) + [(0, 0), (0, GATE_SLAB - MLSTM_HEADS)])
    a = a.reshape(lead + (4 * GATE_SLAB,))
    return jnp.pad(a, [(0, 0)] * len(lead) + [(0, LANES - 4 * GATE_SLAB)])


def _trunk(x, p):
    bsz, seq, _ = x.shape
    xf = x.reshape(bsz * seq, D_MODEL)
    cos_t, sin_t = _rope_tables(seq)
    for i in range(4):
        j = i // 2
        g = p["norm_g"][i][None, :]
        if i % 2 == 0:
            q, kv, gate = _attn_in(xf, g, p["attn_w_in"][j], cos_t, sin_t, seq)
            xf = _attn_core(p["attn_sink"][j], q, kv, gate, xf, p["attn_w_out"][j], seq)
        else:
            q, kt, v, og, colg, rowg, hb = _mlstm_in_bwd(xf, g, p["mlstm_w_main"][j], p["mlstm_w_gate"][j],
                                                         p["mlstm_gate_bias"][j], seq)
            xf = _mlstm_fwd(q, kt, v, colg, rowg, hb, og, p["mlstm_head_norm"][j][None, :], xf, p["mlstm_w_out"][j],
                            p["final_norm_g"][None, :], seq, final_norm=(i == 3))
    return xf.reshape(bsz, seq, D_MODEL)


def kernel(x_prompt, x_sample, norm_g, attn_w_in, attn_sink, attn_w_out, mlstm_w_in, mlstm_gate_bias, mlstm_head_norm, mlstm_w_out, final_norm_g):
    p = {
        "norm_g": norm_g.astype(F32),
        "attn_w_in": attn_w_in.astype(BF16),
        "attn_sink": attn_sink.astype(F32),
        "attn_w_out": attn_w_out.astype(BF16),
        "mlstm_w_main": mlstm_w_in[:, :, :MLSTM_MAIN_IN].astype(BF16),
        "mlstm_w_gate": _gate_slabs(mlstm_w_in[:, :, MLSTM_MAIN_IN:]).astype(BF16),
        "mlstm_gate_bias": _gate_slabs(mlstm_gate_bias.astype(F32))[:, None, :],
        "mlstm_head_norm": mlstm_head_norm.astype(F32),
        "mlstm_w_out": mlstm_w_out.astype(BF16),
        "final_norm_g": final_norm_g.astype(F32),
    }
    return _trunk(x_prompt, p), _trunk(x_sample, p)
```

```python
import functools
import math

import jax
import jax.numpy as jnp
from jax import lax
from jax.experimental import pallas as pl
from jax.experimental.pallas import tpu as pltpu

F32 = jnp.float32
BF16 = jnp.bfloat16

D_MODEL = 1024
EPS = 1e-6
NEG = -1e30
LOG2E = math.log2(math.e)

ATTN_HEADS = 16
ATTN_KV_HEADS = 4
ATTN_HEAD_DIM = 64
ATTN_WIDTH = ATTN_HEADS * ATTN_HEAD_DIM
ATTN_KV_WIDTH = ATTN_KV_HEADS * ATTN_HEAD_DIM
WINDOW = 128
ROPE_THETA = 10000.0
KEY_BLOCK = 128

MLSTM_HEADS = 4
MLSTM_V_DIM = 256
MLSTM_QK_DIM = 128
MLSTM_WIDTH = MLSTM_HEADS * MLSTM_V_DIM
MLSTM_QK_WIDTH = MLSTM_HEADS * MLSTM_QK_DIM
MLSTM_MAIN_IN = 2 * MLSTM_QK_WIDTH + 3 * MLSTM_WIDTH
NEG_INIT = -1e30
CHUNK = 128
GATE_SLAB = 8

LANES = 128
VMEM_LIMIT = 56 * 1024 * 1024

PROJ_TILE = 1024
ATTN_TILE = 1024
MLSTM_TILE = 1024


def _params(n_axes):
    return pltpu.CompilerParams(dimension_semantics=("arbitrary",) * n_axes, vmem_limit_bytes=VMEM_LIMIT)


def _tile(preferred, seq):
    tile = min(preferred, seq)
    assert seq % tile == 0 and tile % LANES == 0, (seq, tile)
    return tile


def _normed_bf16(x_ref, g_ref):
    x = x_ref[...]
    ms = jnp.mean(x * x, axis=-1, keepdims=True)
    return (x * lax.rsqrt(ms + EPS) * g_ref[...]).astype(BF16)


def _attn_in_kernel(x_ref, g_ref, w_ref, cos_ref, sin_ref, q_ref, kv_ref, gate_ref, xn_scr):
    tm = x_ref.shape[0]
    xn_scr[...] = _normed_bf16(x_ref, g_ref)
    cos = cos_ref[...]
    sin = sin_ref[...]
    lane = lax.broadcasted_iota(jnp.int32, (tm, LANES), 1)
    first_half = (lane % ATTN_HEAD_DIM) < (ATTN_HEAD_DIM // 2)

    def rope(a):
        partner = jnp.where(first_half, pltpu.roll(a, LANES - 32, 1), pltpu.roll(a, 32, 1))
        return a * cos + partner * sin

    def proj(col, width):
        return jnp.dot(xn_scr[...], w_ref[:, col:col + width], preferred_element_type=F32)

    scale = ATTN_HEAD_DIM ** -0.5 * LOG2E
    for c in range(ATTN_WIDTH // 256):
        acc = proj(256 * c, 256)
        for j in range(2):
            q_ref[:, 256 * c + LANES * j:256 * c + LANES * (j + 1)] = (
                rope(acc[:, LANES * j:LANES * (j + 1)]) * scale).astype(BF16)
    acc = proj(ATTN_WIDTH, 2 * ATTN_KV_WIDTH)
    for j in range(ATTN_KV_WIDTH // LANES):
        kv_ref[:, LANES * j:LANES * (j + 1)] = rope(acc[:, LANES * j:LANES * (j + 1)]).astype(BF16)
    kv_ref[:, ATTN_KV_WIDTH:] = acc[:, ATTN_KV_WIDTH:].astype(BF16)
    z0 = ATTN_WIDTH + 2 * ATTN_KV_WIDTH
    for c in range(ATTN_WIDTH // 256):
        z = proj(z0 + 256 * c, 256)
        gate_ref[:, 256 * c:256 * (c + 1)] = (z * jax.nn.sigmoid(z)).astype(BF16)


def _attn_in(x, g, w, cos_t, sin_t, seq):
    n = x.shape[0]
    tm = _tile(PROJ_TILE, seq)
    tiles_per_seq = seq // tm
    row = lambda i: (i, 0)
    const = lambda i: (0, 0)
    pos = lambda i: (i % tiles_per_seq, 0)
    return pl.pallas_call(
        _attn_in_kernel,
        grid=(n // tm,),
        in_specs=[pl.BlockSpec((tm, D_MODEL), row), pl.BlockSpec((1, D_MODEL), const),
                  pl.BlockSpec(w.shape, const, pipeline_mode=pl.Buffered(1)),
                  pl.BlockSpec((tm, LANES), pos), pl.BlockSpec((tm, LANES), pos)],
        out_specs=[pl.BlockSpec((tm, ATTN_WIDTH), row), pl.BlockSpec((tm, 2 * ATTN_KV_WIDTH), row),
                   pl.BlockSpec((tm, ATTN_WIDTH), row)],
        out_shape=[jax.ShapeDtypeStruct((n, ATTN_WIDTH), BF16), jax.ShapeDtypeStruct((n, 2 * ATTN_KV_WIDTH), BF16),
                   jax.ShapeDtypeStruct((n, ATTN_WIDTH), BF16)],
        scratch_shapes=[pltpu.VMEM((tm, D_MODEL), BF16)],
        compiler_params=_params(1),
        name="attn_in",
    )(x, g, w, cos_t, sin_t)


def _attn_core_kernel(sink_ref, q_ref, kv_ref, kvp_ref, kvn_ref, gate_ref, x_ref, wo_ref, out_ref,
                      kx_scr, vx_scr, o_scr):
    t = pl.program_id(1)
    n_t = pl.num_programs(1)
    tq = q_ref.shape[0]
    nblk = tq // KEY_BLOCK
    kb = KEY_BLOCK
    half = ATTN_HEAD_DIM

    lane = lax.broadcasted_iota(jnp.int32, (kb, LANES), 1)
    low = lane < half
    ones_lo = jnp.where(low, 1.0, 0.0).astype(BF16)
    ones_hi = jnp.where(low, 0.0, 1.0).astype(BF16)

    def expand(blk_ref, r0, jb):
        for vi in range(ATTN_KV_WIDTH // LANES):
            kcol = blk_ref[r0:r0 + kb, LANES * vi:LANES * (vi + 1)].astype(F32)
            vcol = blk_ref[r0:r0 + kb, ATTN_KV_WIDTH + LANES * vi:ATTN_KV_WIDTH + LANES * (vi + 1)].astype(F32)
            kswap = pltpu.roll(kcol, half, 1)
            vswap = pltpu.roll(vcol, half, 1)
            for hf in range(2):
                g = 2 * vi + hf
                k_src_lo, k_src_hi = (kcol, kswap) if hf == 0 else (kswap, kcol)
                v_src_lo, v_src_hi = (vcol, vswap) if hf == 0 else (vswap, vcol)
                kx_scr[g, jb, 0] = jnp.where(low, k_src_lo, 0.0).astype(BF16)
                kx_scr[g, jb, 1] = jnp.where(low, 0.0, k_src_hi).astype(BF16)
                vx_scr[g, jb, 0, :, :LANES] = jnp.where(low, v_src_lo, 0.0).astype(BF16)
                vx_scr[g, jb, 0, :, LANES:] = ones_lo
                vx_scr[g, jb, 1, :, :LANES] = jnp.where(low, 0.0, v_src_hi).astype(BF16)
                vx_scr[g, jb, 1, :, LANES:] = ones_hi

    expand(kvp_ref, 0, 0)
    for jb in range(nblk):
        expand(kv_ref, kb * jb, jb + 1)
    expand(kvn_ref, 0, nblk + 1)

    qi = lax.broadcasted_iota(jnp.int32, (kb, kb), 0)
    ki = lax.broadcasted_iota(jnp.int32, (kb, kb), 1)
    band_prev = jnp.where(ki >= qi, 0.0, NEG).astype(F32)
    band_next = jnp.where(ki <= qi, 0.0, NEG).astype(F32)

    for blk in range(nblk):
        r0 = kb * blk
        bias_prev = jnp.where(t == 0, NEG, band_prev) if blk == 0 else band_prev
        bias_next = jnp.where(t == n_t - 1, NEG, band_next) if blk == nblk - 1 else band_next
        for g in range(ATTN_KV_HEADS):
            c0 = 2 * LANES * g
            q2 = jnp.concatenate([q_ref[r0:r0 + kb, c0:c0 + LANES],
                                  q_ref[r0:r0 + kb, c0 + LANES:c0 + 2 * LANES]], axis=0)
            kx = kx_scr[g, blk:blk + 3].reshape(6 * kb, LANES)
            s = lax.dot_general(q2, kx, (((1,), (1,)), ((), ())), preferred_element_type=F32)
            p_rows = []
            sink_terms = []
            for p in range(2):
                cols = [None] * 6
                maxes = []
                sinks = []
                for ab in range(2):
                    head = 4 * g + 2 * p + ab
                    sink = sink_ref[head] * LOG2E
                    sp = s[kb * p:kb * (p + 1), kb * ab:kb * (ab + 1)] + bias_prev
                    so = s[kb * p:kb * (p + 1), kb * (2 + ab):kb * (3 + ab)]
                    sn = s[kb * p:kb * (p + 1), kb * (4 + ab):kb * (5 + ab)] + bias_next
                    m = jnp.max(jnp.maximum(jnp.maximum(sp, so), sn), axis=1, keepdims=True)
                    m = jnp.maximum(m, sink)
                    cols[ab] = jnp.exp2(sp - m).astype(BF16)
                    cols[2 + ab] = jnp.exp2(so - m).astype(BF16)
                    cols[4 + ab] = jnp.exp2(sn - m).astype(BF16)
                    maxes.append(m)
                    sinks.append(sink)
                p_rows.append(jnp.concatenate(cols, axis=1))
                sink_terms.append(jnp.exp2(jnp.where(low, sinks[0], sinks[1]) - jnp.where(low, maxes[0], maxes[1])))
            pmat = jnp.concatenate(p_rows, axis=0)
            vx = vx_scr[g, blk:blk + 3].reshape(6 * kb, 2 * LANES)
            r = jnp.dot(pmat, vx, preferred_element_type=F32)
            for p in range(2):
                num = r[kb * p:kb * (p + 1), :LANES]
                den = r[kb * p:kb * (p + 1), LANES:] + sink_terms[p]
                o_scr[r0:r0 + kb, c0 + LANES * p:c0 + LANES * (p + 1)] = num / den

    y = (o_scr[...] * gate_ref[...].astype(F32)).astype(BF16)
    out_ref[...] = x_ref[...] + jnp.dot(y, wo_ref[...], preferred_element_type=F32)


def _attn_core(sink, q, kv, gate, x, wo, seq):
    n = x.shape[0]
    tq = _tile(ATTN_TILE, seq)
    n_t = seq // tq
    nblk = tq // KEY_BLOCK
    last_blk = n // KEY_BLOCK - 1
    row = lambda b, t: (b * n_t + t, 0)
    const = lambda b, t: (0, 0)
    prev = lambda b, t: (jnp.maximum((b * n_t + t) * nblk - 1, 0), 0)
    nxt = lambda b, t: (jnp.minimum((b * n_t + t + 1) * nblk, last_blk), 0)
    return pl.pallas_call(
        _attn_core_kernel,
        grid=(n // seq, n_t),
        in_specs=[pl.BlockSpec(memory_space=pltpu.SMEM),
                  pl.BlockSpec((tq, ATTN_WIDTH), row), pl.BlockSpec((tq, 2 * ATTN_KV_WIDTH), row),
                  pl.BlockSpec((KEY_BLOCK, 2 * ATTN_KV_WIDTH), prev), pl.BlockSpec((KEY_BLOCK, 2 * ATTN_KV_WIDTH), nxt),
                  pl.BlockSpec((tq, ATTN_WIDTH), row), pl.BlockSpec((tq, D_MODEL), row),
                  pl.BlockSpec((ATTN_WIDTH, D_MODEL), const)],
        out_specs=pl.BlockSpec((tq, D_MODEL), row),
        out_shape=jax.ShapeDtypeStruct((n, D_MODEL), F32),
        scratch_shapes=[pltpu.VMEM((ATTN_KV_HEADS, nblk + 2, 2, KEY_BLOCK, LANES), BF16),
                        pltpu.VMEM((ATTN_KV_HEADS, nblk + 2, 2, KEY_BLOCK, 2 * LANES), BF16),
                        pltpu.VMEM((tq, ATTN_WIDTH), F32)],
        compiler_params=_params(2),
        name="attn_core",
    )(sink, q, kv, kv, kv, gate, x, wo)


ROW_SLABS = 6


def _log_sigmoid(x):
    return jnp.minimum(x, 0.0) - jnp.log1p(jnp.exp(-jnp.abs(x)))


def _segment_scan(x, op, fill, pos_in_chunk, reverse):
    width = x.shape[1]
    d = 1
    while d < CHUNK:
        if reverse:
            shifted = pltpu.roll(x, width - d, 1)
            valid = pos_in_chunk < CHUNK - d
        else:
            shifted = pltpu.roll(x, d, 1)
            valid = pos_in_chunk >= d
        x = op(x, jnp.where(valid, shifted, fill))
        d *= 2
    return x


def _mlstm_project(x_ref, g_ref, w_ref, wg_ref, bias_ref, q_ref, kt_ref, v_ref, og_ref, colg_ref, rowg_ref, xn_scr):
    tm = x_ref.shape[0]
    xn_scr[...] = _normed_bf16(x_ref, g_ref)

    def proj(col, width):
        return jnp.dot(xn_scr[...], w_ref[:, col:col + width], preferred_element_type=F32)

    gates = jnp.dot(xn_scr[...], wg_ref[...], preferred_element_type=F32) + bias_ref[...]
    gt = gates.T
    s = GATE_SLAB
    ig_f, fg_f, ig_b, fg_b = gt[0:s], gt[s:2 * s], gt[2 * s:3 * s], gt[3 * s:4 * s]
    pos = lax.broadcasted_iota(jnp.int32, (s, tm), 1) % CHUNK
    b_f = _segment_scan(_log_sigmoid(fg_f) * LOG2E, jnp.add, 0.0, pos, False)
    u_f = ig_f * LOG2E - b_f
    cu_f = _segment_scan(u_f, jnp.maximum, -jnp.inf, pos, False)
    b_b = _segment_scan(_log_sigmoid(fg_b) * LOG2E, jnp.add, 0.0, pos, True)
    u_b = ig_b * LOG2E - b_b
    cu_b = _segment_scan(u_b, jnp.maximum, -jnp.inf, pos, True)
    pad = jnp.zeros((LANES - 4 * s, tm), F32)
    colg_ref[...] = jnp.concatenate([-b_f, cu_f, -b_b, cu_b, pad], axis=0).T
    rowg_ref[...] = jnp.concatenate([u_f, u_b, b_f, cu_f, b_b, cu_b], axis=0)

    for c in range(MLSTM_QK_WIDTH // 256):
        q_ref[:, 256 * c:256 * (c + 1)] = proj(256 * c, 256).astype(BF16)
    kscale = MLSTM_QK_DIM ** -0.5
    for c in range(MLSTM_QK_WIDTH // 256):
        kt_ref[256 * c:256 * (c + 1), :] = (proj(MLSTM_QK_WIDTH + 256 * c, 256) * kscale).T.astype(BF16)
    v0 = 2 * MLSTM_QK_WIDTH
    for c in range(MLSTM_WIDTH // 256):
        v_ref[:, 256 * c:256 * (c + 1)] = proj(v0 + 256 * c, 256).astype(BF16)
    o0 = v0 + MLSTM_WIDTH
    z0 = o0 + MLSTM_WIDTH

    def gate_chunk(c):
        o = proj(o0 + 256 * c, 256)
        z = proj(z0 + 256 * c, 256)
        og_ref[:, 256 * c:256 * (c + 1)] = (jax.nn.sigmoid(o) * (z * jax.nn.sigmoid(z))).astype(BF16)

    return [functools.partial(gate_chunk, c) for c in range(MLSTM_WIDTH // 256)]


def _mlstm_sweep_tile(q_ref, kt_ref, v_ref, colg_ref, rowg_ref, c_scr, n_scr, m_scr, h_scr, vc_scr,
                      reverse, chunk_done=None, fillers=()):
    assert CHUNK == LANES
    tt = q_ref.shape[0]
    nch = tt // CHUNK
    L = CHUNK
    ti = lax.broadcasted_iota(jnp.int32, (L, L), 0)
    si = lax.broadcasted_iota(jnp.int32, (L, L), 1)
    causal = (si >= ti) if reverse else (si <= ti)
    col0 = 2 * GATE_SLAB if reverse else 0
    urow0 = GATE_SLAB if reverse else 0
    half = MLSTM_V_DIM // 2
    for slot in range(nch * MLSTM_HEADS):
        vc_scr[slot, 0:L, MLSTM_V_DIM:] = jnp.ones((L, LANES), BF16)

    chunks = range(nch - 1, -1, -1) if reverse else range(nch)

    brow0 = (4 if reverse else 2) * GATE_SLAB
    m_prev, m_ref = {}, {}
    m = m_scr[...]
    for c in chunks:
        end = L * c if reverse else L * c + L - 1
        b_end = jnp.broadcast_to(rowg_ref[brow0:brow0 + GATE_SLAB, end:end + 1], (GATE_SLAB, LANES))
        cu_end = jnp.broadcast_to(rowg_ref[brow0 + GATE_SLAB:brow0 + 2 * GATE_SLAB, end:end + 1], (GATE_SLAB, LANES))
        m_new_minus_b = jnp.maximum(m, cu_end)
        for h in range(MLSTM_HEADS):
            m_prev[c, h] = m[h:h + 1, :]
            m_ref[c, h] = m_new_minus_b[h:h + 1, :]
        m = m_new_minus_b + b_end
    m_scr[...] = m

    fillers = list(fillers)
    lhs, floor, dr = {}, {}, {}
    for c in chunks:
        r0 = L * c
        cg = colg_ref[r0:r0 + L, :]
        if fillers:
            fillers.pop(0)()
        for h in range(MLSTM_HEADS):
            slot = c * MLSTM_HEADS + h
            qh = q_ref[r0:r0 + L, MLSTM_QK_DIM * h:MLSTM_QK_DIM * (h + 1)]
            kth = kt_ref[MLSTM_QK_DIM * h:MLSTM_QK_DIM * (h + 1), r0:r0 + L]
            nb_rep = jnp.broadcast_to(cg[:, col0 + h:col0 + h + 1], (L, LANES))
            cu_rep = jnp.broadcast_to(cg[:, col0 + GATE_SLAB + h:col0 + GATE_SLAB + h + 1], (L, LANES))
            u_r = rowg_ref[urow0 + h:urow0 + h + 1, r0:r0 + L]
            mx = jnp.maximum(cu_rep, m_prev[c, h])
            e = jnp.exp2(jnp.where(causal, u_r - mx, NEG))
            s = jnp.dot(qh, kth, preferred_element_type=F32)
            sc = jnp.exp2(m_prev[c, h] - mx)
            lhs[c, h] = jnp.concatenate([(e * s).astype(BF16), qh * sc.astype(BF16)], axis=1)
            floor[c, h] = jnp.exp2(nb_rep - mx)
            vc_scr[slot, 0:L, 0:MLSTM_V_DIM] = v_ref[r0:r0 + L, MLSTM_V_DIM * h:MLSTM_V_DIM * (h + 1)]

    pending = None
    for c in chunks:
        r0 = L * c
        for h in range(MLSTM_HEADS):
            slot = c * MLSTM_HEADS + h
            kth = kt_ref[MLSTM_QK_DIM * h:MLSTM_QK_DIM * (h + 1), r0:r0 + L]
            u_r = rowg_ref[urow0 + h:urow0 + h + 1, r0:r0 + L]
            kwt = kth * jnp.exp2(u_r - m_ref[c, h]).astype(BF16)
            dr[c, h] = jnp.dot(kwt, vc_scr[slot, 0:L, :], preferred_element_type=F32)
        for h in range(MLSTM_HEADS):
            slot = c * MLSTM_HEADS + h
            vc_scr[slot, L:, 0:MLSTM_V_DIM] = c_scr[h].astype(BF16)
            vc_scr[slot, L:, MLSTM_V_DIM:] = n_scr[h].astype(BF16)
            r = jnp.dot(lhs[c, h], vc_scr[slot], preferred_element_type=F32)
            inv = 1.0 / jnp.maximum(jnp.abs(r[:, MLSTM_V_DIM:]), floor[c, h])
            c0 = MLSTM_V_DIM * h
            h_scr[r0:r0 + L, c0:c0 + half] = r[:, :half] * inv
            h_scr[r0:r0 + L, c0 + half:c0 + 2 * half] = r[:, half:2 * half] * inv
            sp = jnp.exp2(m_prev[c, h] - m_ref[c, h])
            c_scr[h] = jnp.concatenate([sp, sp], axis=1) * c_scr[h] + dr[c, h][:, :MLSTM_V_DIM]
            n_scr[h] = sp * n_scr[h] + dr[c, h][:, MLSTM_V_DIM:]
        if chunk_done is not None:
            if pending is not None:
                chunk_done(pending)
            pending = r0
    if pending is not None:
        chunk_done(pending)
    for filler in fillers:
        filler()


def _reset_state(c_scr, n_scr, m_scr):
    c_scr[...] = jnp.zeros(c_scr.shape, F32)
    n_scr[...] = jnp.zeros(n_scr.shape, F32)
    m_scr[...] = jnp.full(m_scr.shape, NEG_INIT, F32)


def _mlstm_in_bwd_kernel(x_ref, g_ref, w_ref, wg_ref, bias_ref, q_ref, kt_ref, v_ref, og_ref, colg_ref, rowg_ref,
                         hb_ref, xn_scr, c_scr, n_scr, m_scr, h_scr, vc_scr):
    @pl.when(pl.program_id(1) == 0)
    def _():
        _reset_state(c_scr, n_scr, m_scr)

    gate_chunks = _mlstm_project(x_ref, g_ref, w_ref, wg_ref, bias_ref, q_ref, kt_ref, v_ref, og_ref, colg_ref,
                                 rowg_ref, xn_scr)
    _mlstm_sweep_tile(q_ref, kt_ref, v_ref, colg_ref, rowg_ref, c_scr, n_scr, m_scr, h_scr, vc_scr, reverse=True,
                      fillers=gate_chunks)
    hb_ref[...] = h_scr[...].astype(BF16)


def _mlstm_fwd_kernel(q_ref, kt_ref, v_ref, colg_ref, rowg_ref, hb_ref, og_ref, hn_ref, x_ref, wo_ref,
                      fg_ref, out_ref, c_scr, n_scr, m_scr, h_scr, vc_scr, *, final_norm):
    @pl.when(pl.program_id(1) == 0)
    def _():
        _reset_state(c_scr, n_scr, m_scr)

    def finish_rows(r0):
        rows = slice(r0, r0 + CHUNK)
        for h in range(MLSTM_HEADS):
            cs = slice(MLSTM_V_DIM * h, MLSTM_V_DIM * (h + 1))
            hh = h_scr[rows, cs] + hb_ref[rows, cs].astype(F32)
            ms = jnp.mean(hh * hh, axis=-1, keepdims=True)
            hh = hh * lax.rsqrt(ms + EPS) * hn_ref[:, cs]
            h_scr[rows, cs] = hh * og_ref[rows, cs].astype(F32)
        y = x_ref[rows, :] + jnp.dot(h_scr[rows, :].astype(BF16), wo_ref[...], preferred_element_type=F32)
        if final_norm:
            ms = jnp.mean(y * y, axis=-1, keepdims=True)
            y = y * lax.rsqrt(ms + EPS) * fg_ref[...]
        out_ref[rows, :] = y

    _mlstm_sweep_tile(q_ref, kt_ref, v_ref, colg_ref, rowg_ref, c_scr, n_scr, m_scr, h_scr, vc_scr, reverse=False,
                      chunk_done=finish_rows)


def _mlstm_state_scratch(tt):
    return [pltpu.VMEM((MLSTM_HEADS, MLSTM_QK_DIM, MLSTM_V_DIM), F32),
            pltpu.VMEM((MLSTM_HEADS, MLSTM_QK_DIM, LANES), F32),
            pltpu.VMEM((GATE_SLAB, LANES), F32),
            pltpu.VMEM((tt, MLSTM_WIDTH), F32),
            pltpu.VMEM((tt // CHUNK * MLSTM_HEADS, CHUNK + MLSTM_QK_DIM, MLSTM_V_DIM + LANES), BF16)]


def _mlstm_in_bwd(x, g, w, wg, bias, seq):
    n = x.shape[0]
    tt = _tile(MLSTM_TILE, seq)
    n_t = seq // tt
    const = lambda b, t: (0, 0)
    row = lambda b, t: (b * n_t + n_t - 1 - t, 0)
    col = lambda b, t: (0, b * n_t + n_t - 1 - t)
    return pl.pallas_call(
        _mlstm_in_bwd_kernel,
        grid=(n // seq, n_t),
        in_specs=[pl.BlockSpec((tt, D_MODEL), row), pl.BlockSpec((1, D_MODEL), const),
                  pl.BlockSpec(w.shape, const, pipeline_mode=pl.Buffered(1)), pl.BlockSpec(wg.shape, const),
                  pl.BlockSpec((1, LANES), const)],
        out_specs=[pl.BlockSpec((tt, MLSTM_QK_WIDTH), row), pl.BlockSpec((MLSTM_QK_WIDTH, tt), col),
                   pl.BlockSpec((tt, MLSTM_WIDTH), row), pl.BlockSpec((tt, MLSTM_WIDTH), row),
                   pl.BlockSpec((tt, LANES), row), pl.BlockSpec((ROW_SLABS * GATE_SLAB, tt), col),
                   pl.BlockSpec((tt, MLSTM_WIDTH), row)],
        out_shape=[jax.ShapeDtypeStruct((n, MLSTM_QK_WIDTH), BF16), jax.ShapeDtypeStruct((MLSTM_QK_WIDTH, n), BF16),
                   jax.ShapeDtypeStruct((n, MLSTM_WIDTH), BF16), jax.ShapeDtypeStruct((n, MLSTM_WIDTH), BF16),
                   jax.ShapeDtypeStruct((n, LANES), F32), jax.ShapeDtypeStruct((ROW_SLABS * GATE_SLAB, n), F32),
                   jax.ShapeDtypeStruct((n, MLSTM_WIDTH), BF16)],
        scratch_shapes=[pltpu.VMEM((tt, D_MODEL), BF16)] + _mlstm_state_scratch(tt),
        compiler_params=_params(2),
        name="mlstm_in_bwd",
    )(x, g, w, wg, bias)


def _mlstm_fwd(q, kt, v, colg, rowg, hb, og, hn, x, wo, fg, seq, final_norm):
    n = x.shape[0]
    tt = _tile(MLSTM_TILE, seq)
    n_t = seq // tt
    const = lambda b, t: (0, 0)
    row = lambda b, t: (b * n_t + t, 0)
    col = lambda b, t: (0, b * n_t + t)
    return pl.pallas_call(
        functools.partial(_mlstm_fwd_kernel, final_norm=final_norm),
        grid=(n // seq, n_t),
        in_specs=[pl.BlockSpec((tt, MLSTM_QK_WIDTH), row), pl.BlockSpec((MLSTM_QK_WIDTH, tt), col),
                  pl.BlockSpec((tt, MLSTM_WIDTH), row), pl.BlockSpec((tt, LANES), row),
                  pl.BlockSpec((ROW_SLABS * GATE_SLAB, tt), col),
                  pl.BlockSpec((tt, MLSTM_WIDTH), row), pl.BlockSpec((tt, MLSTM_WIDTH), row),
                  pl.BlockSpec((1, MLSTM_WIDTH), const), pl.BlockSpec((tt, D_MODEL), row),
                  pl.BlockSpec((MLSTM_WIDTH, D_MODEL), const), pl.BlockSpec((1, D_MODEL), const)],
        out_specs=pl.BlockSpec((tt, D_MODEL), row),
        out_shape=jax.ShapeDtypeStruct((n, D_MODEL), F32),
        scratch_shapes=_mlstm_state_scratch(tt),
        compiler_params=_params(2),
        name="mlstm_fwd",
    )(q, kt, v, colg, rowg, hb, og, hn, x, wo, fg)


def _rope_tables(seq):
    half = ATTN_HEAD_DIM // 2
    inv = jnp.exp(-math.log(ROPE_THETA) * jnp.arange(half, dtype=F32) / half)
    ang = jnp.arange(seq).astype(F32)[:, None] * inv[None, :]
    cos = jnp.cos(ang)
    sin = jnp.sin(ang)
    cos_t = jnp.tile(cos, (1, LANES // half))
    sin_t = jnp.tile(jnp.concatenate([-sin, sin], axis=1), (1, LANES // ATTN_HEAD_DIM))
    return cos_t, sin_t


def _gate_slabs(a):
    lead = a.shape[:-1]
    a = a.reshape(lead + (4, MLSTM_HEADS))
    a = jnp.pad(a, [(0, 0)] * len(lead) + [(0, 0), (0, GATE_SLAB - MLSTM_HEADS)])
    a = a.reshape(lead + (4 * GATE_SLAB,))
    return jnp.pad(a, [(0, 0)] * len(lead) + [(0, LANES - 4 * GATE_SLAB)])


def _trunk(x, p):
    bsz, seq, _ = x.shape
    xf = x.reshape(bsz * seq, D_MODEL)
    cos_t, sin_t = _rope_tables(seq)
    for i in range(4):
        j = i // 2
        g = p["norm_g"][i][None, :]
        if i % 2 == 0:
            q, kv, gate = _attn_in(xf, g, p["attn_w_in"][j], cos_t, sin_t, seq)
            xf = _attn_core(p["attn_sink"][j], q, kv, gate, xf, p["attn_w_out"][j], seq)
        else:
            q, kt, v, og, colg, rowg, hb = _mlstm_in_bwd(xf, g, p["mlstm_w_main"][j], p["mlstm_w_gate"][j],
                                                         p["mlstm_gate_bias"][j], seq)
            xf = _mlstm_fwd(q, kt, v, colg, rowg, hb, og, p["mlstm_head_norm"][j][None, :], xf, p["mlstm_w_out"][j],
                            p["final_norm_g"][None, :], seq, final_norm=(i == 3))
    return xf.reshape(bsz, seq, D_MODEL)


def kernel(x_prompt, x_sample, norm_g, attn_w_in, attn_sink, attn_w_out, mlstm_w_in, mlstm_gate_bias, mlstm_head_norm, mlstm_w_out, final_norm_g):
    p = {
        "norm_g": norm_g.astype(F32),
        "attn_w_in": attn_w_in.astype(BF16),
        "attn_sink": attn_sink.astype(F32),
        "attn_w_out": attn_w_out.astype(BF16),
        "mlstm_w_main": mlstm_w_in[:, :, :MLSTM_MAIN_IN].astype(BF16),
        "mlstm_w_gate": _gate_slabs(mlstm_w_in[:, :, MLSTM_MAIN_IN:]).astype(BF16),
        "mlstm_gate_bias": _gate_slabs(mlstm_gate_bias.astype(F32))[:, None, :],
        "mlstm_head_norm": mlstm_head_norm.astype(F32),
        "mlstm_w_out": mlstm_w_out.astype(BF16),
        "final_norm_g": final_norm_g.astype(F32),
    }
    return _trunk(x_prompt, p), _trunk(x_sample, p)
```

```python
import functools
import math

import jax
import jax.numpy as jnp
from jax import lax
from jax.experimental import pallas as pl
from jax.experimental.pallas import tpu as pltpu

F32 = jnp.float32
BF16 = jnp.bfloat16

D_MODEL = 1024
EPS = 1e-6
NEG = -1e30
LOG2E = math.log2(math.e)

ATTN_HEADS = 16
ATTN_KV_HEADS = 4
ATTN_HEAD_DIM = 64
ATTN_WIDTH = ATTN_HEADS * ATTN_HEAD_DIM
ATTN_KV_WIDTH = ATTN_KV_HEADS * ATTN_HEAD_DIM
WINDOW = 128
ROPE_THETA = 10000.0
KEY_BLOCK = 128

MLSTM_HEADS = 4
MLSTM_V_DIM = 256
MLSTM_QK_DIM = 128
MLSTM_WIDTH = MLSTM_HEADS * MLSTM_V_DIM
MLSTM_QK_WIDTH = MLSTM_HEADS * MLSTM_QK_DIM
MLSTM_MAIN_IN = 2 * MLSTM_QK_WIDTH + 3 * MLSTM_WIDTH
NEG_INIT = -1e30
CHUNK = 128
GATE_SLAB = 8
SCORE_LOOKAHEAD = 2

LANES = 128
VMEM_LIMIT = 56 * 1024 * 1024

PROJ_TILE = 1024
ATTN_TILE = 1024
MLSTM_TILE = 1024


def _params(n_axes):
    return pltpu.CompilerParams(dimension_semantics=("arbitrary",) * n_axes, vmem_limit_bytes=VMEM_LIMIT)


def _tile(preferred, seq):
    tile = min(preferred, seq)
    assert seq % tile == 0 and tile % LANES == 0, (seq, tile)
    return tile


def _normed_bf16(x_ref, g_ref):
    x = x_ref[...]
    ms = jnp.mean(x * x, axis=-1, keepdims=True)
    return (x * lax.rsqrt(ms + EPS) * g_ref[...]).astype(BF16)


def _attn_in_kernel(x_ref, g_ref, w_ref, cos_ref, sin_ref, q_ref, kv_ref, gate_ref, xn_scr):
    tm = x_ref.shape[0]
    xn_scr[...] = _normed_bf16(x_ref, g_ref)
    cos = cos_ref[...]
    sin = sin_ref[...]
    lane = lax.broadcasted_iota(jnp.int32, (tm, LANES), 1)
    first_half = (lane % ATTN_HEAD_DIM) < (ATTN_HEAD_DIM // 2)

    def rope(a):
        partner = jnp.where(first_half, pltpu.roll(a, LANES - 32, 1), pltpu.roll(a, 32, 1))
        return a * cos + partner * sin

    def proj(col, width):
        return jnp.dot(xn_scr[...], w_ref[:, col:col + width], preferred_element_type=F32)

    scale = ATTN_HEAD_DIM ** -0.5 * LOG2E
    for c in range(ATTN_WIDTH // 256):
        acc = proj(256 * c, 256)
        for j in range(2):
            q_ref[:, 256 * c + LANES * j:256 * c + LANES * (j + 1)] = (
                rope(acc[:, LANES * j:LANES * (j + 1)]) * scale).astype(BF16)
    acc = proj(ATTN_WIDTH, 2 * ATTN_KV_WIDTH)
    for j in range(ATTN_KV_WIDTH // LANES):
        kv_ref[:, LANES * j:LANES * (j + 1)] = rope(acc[:, LANES * j:LANES * (j + 1)]).astype(BF16)
    kv_ref[:, ATTN_KV_WIDTH:] = acc[:, ATTN_KV_WIDTH:].astype(BF16)
    z0 = ATTN_WIDTH + 2 * ATTN_KV_WIDTH
    for c in range(ATTN_WIDTH // 256):
        z = proj(z0 + 256 * c, 256)
        gate_ref[:, 256 * c:256 * (c + 1)] = (z * jax.nn.sigmoid(z)).astype(BF16)


def _attn_in(x, g, w, cos_t, sin_t, seq):
    n = x.shape[0]
    tm = _tile(PROJ_TILE, seq)
    tiles_per_seq = seq // tm
    row = lambda i: (i, 0)
    const = lambda i: (0, 0)
    pos = lambda i: (i % tiles_per_seq, 0)
    return pl.pallas_call(
        _attn_in_kernel,
        grid=(n // tm,),
        in_specs=[pl.BlockSpec((tm, D_MODEL), row), pl.BlockSpec((1, D_MODEL), const),
                  pl.BlockSpec(w.shape, const, pipeline_mode=pl.Buffered(1)),
                  pl.BlockSpec((tm, LANES), pos), pl.BlockSpec((tm, LANES), pos)],
        out_specs=[pl.BlockSpec((tm, ATTN_WIDTH), row), pl.BlockSpec((tm, 2 * ATTN_KV_WIDTH), row),
                   pl.BlockSpec((tm, ATTN_WIDTH), row)],
        out_shape=[jax.ShapeDtypeStruct((n, ATTN_WIDTH), BF16), jax.ShapeDtypeStruct((n, 2 * ATTN_KV_WIDTH), BF16),
                   jax.ShapeDtypeStruct((n, ATTN_WIDTH), BF16)],
        scratch_shapes=[pltpu.VMEM((tm, D_MODEL), BF16)],
        compiler_params=_params(1),
        name="attn_in",
    )(x, g, w, cos_t, sin_t)


def _attn_core_kernel(sink_ref, q_ref, kv_ref, kvp_ref, kvn_ref, gate_ref, x_ref, wo_ref, out_ref,
                      kx_scr, vx_scr, o_scr):
    t = pl.program_id(1)
    n_t = pl.num_programs(1)
    tq = q_ref.shape[0]
    nblk = tq // KEY_BLOCK
    kb = KEY_BLOCK
    half = ATTN_HEAD_DIM

    lane = lax.broadcasted_iota(jnp.int32, (kb, LANES), 1)
    low = lane < half
    ones_lo = jnp.where(low, 1.0, 0.0).astype(BF16)
    ones_hi = jnp.where(low, 0.0, 1.0).astype(BF16)

    def expand(blk_ref, r0, jb):
        for vi in range(ATTN_KV_WIDTH // LANES):
            kcol = blk_ref[r0:r0 + kb, LANES * vi:LANES * (vi + 1)].astype(F32)
            vcol = blk_ref[r0:r0 + kb, ATTN_KV_WIDTH + LANES * vi:ATTN_KV_WIDTH + LANES * (vi + 1)].astype(F32)
            kswap = pltpu.roll(kcol, half, 1)
            vswap = pltpu.roll(vcol, half, 1)
            for hf in range(2):
                g = 2 * vi + hf
                k_src_lo, k_src_hi = (kcol, kswap) if hf == 0 else (kswap, kcol)
                v_src_lo, v_src_hi = (vcol, vswap) if hf == 0 else (vswap, vcol)
                kx_scr[g, jb, 0] = jnp.where(low, k_src_lo, 0.0).astype(BF16)
                kx_scr[g, jb, 1] = jnp.where(low, 0.0, k_src_hi).astype(BF16)
                vx_scr[g, jb, 0, :, :LANES] = jnp.where(low, v_src_lo, 0.0).astype(BF16)
                vx_scr[g, jb, 0, :, LANES:] = ones_lo
                vx_scr[g, jb, 1, :, :LANES] = jnp.where(low, 0.0, v_src_hi).astype(BF16)
                vx_scr[g, jb, 1, :, LANES:] = ones_hi

    expand(kvp_ref, 0, 0)
    for jb in range(nblk):
        expand(kv_ref, kb * jb, jb + 1)
    expand(kvn_ref, 0, nblk + 1)

    qi = lax.broadcasted_iota(jnp.int32, (kb, kb), 0)
    ki = lax.broadcasted_iota(jnp.int32, (kb, kb), 1)
    band_prev = jnp.where(ki >= qi, 0.0, NEG).astype(F32)
    band_next = jnp.where(ki <= qi, 0.0, NEG).astype(F32)

    for blk in range(nblk):
        r0 = kb * blk
        bias_prev = jnp.where(t == 0, NEG, band_prev) if blk == 0 else band_prev
        bias_next = jnp.where(t == n_t - 1, NEG, band_next) if blk == nblk - 1 else band_next
        for g in range(ATTN_KV_HEADS):
            c0 = 2 * LANES * g
            q2 = jnp.concatenate([q_ref[r0:r0 + kb, c0:c0 + LANES],
                                  q_ref[r0:r0 + kb, c0 + LANES:c0 + 2 * LANES]], axis=0)
            kx = kx_scr[g, blk:blk + 3].reshape(6 * kb, LANES)
            s = lax.dot_general(q2, kx, (((1,), (1,)), ((), ())), preferred_element_type=F32)
            p_rows = []
            sink_terms = []
            for p in range(2):
                cols = [None] * 6
                maxes = []
                sinks = []
                for ab in range(2):
                    head = 4 * g + 2 * p + ab
                    sink = sink_ref[head] * LOG2E
                    sp = s[kb * p:kb * (p + 1), kb * ab:kb * (ab + 1)] + bias_prev
                    so = s[kb * p:kb * (p + 1), kb * (2 + ab):kb * (3 + ab)]
                    sn = s[kb * p:kb * (p + 1), kb * (4 + ab):kb * (5 + ab)] + bias_next
                    m = jnp.max(jnp.maximum(jnp.maximum(sp, so), sn), axis=1, keepdims=True)
                    m = jnp.maximum(m, sink)
                    cols[ab] = jnp.exp2(sp - m).astype(BF16)
                    cols[2 + ab] = jnp.exp2(so - m).astype(BF16)
                    cols[4 + ab] = jnp.exp2(sn - m).astype(BF16)
                    maxes.append(m)
                    sinks.append(sink)
                p_rows.append(jnp.concatenate(cols, axis=1))
                sink_terms.append(jnp.exp2(jnp.where(low, sinks[0], sinks[1]) - jnp.where(low, maxes[0], maxes[1])))
            pmat = jnp.concatenate(p_rows, axis=0)
            vx = vx_scr[g, blk:blk + 3].reshape(6 * kb, 2 * LANES)
            r = jnp.dot(pmat, vx, preferred_element_type=F32)
            for p in range(2):
                num = r[kb * p:kb * (p + 1), :LANES]
                den = r[kb * p:kb * (p + 1), LANES:] + sink_terms[p]
                o_scr[r0:r0 + kb, c0 + LANES * p:c0 + LANES * (p + 1)] = num / den

    y = (o_scr[...] * gate_ref[...].astype(F32)).astype(BF16)
    out_ref[...] = x_ref[...] + jnp.dot(y, wo_ref[...], preferred_element_type=F32)


def _attn_core(sink, q, kv, gate, x, wo, seq):
    n = x.shape[0]
    tq = _tile(ATTN_TILE, seq)
    n_t = seq // tq
    nblk = tq // KEY_BLOCK
    last_blk = n // KEY_BLOCK - 1
    row = lambda b, t: (b * n_t + t, 0)
    const = lambda b, t: (0, 0)
    prev = lambda b, t: (jnp.maximum((b * n_t + t) * nblk - 1, 0), 0)
    nxt = lambda b, t: (jnp.minimum((b * n_t + t + 1) * nblk, last_blk), 0)
    return pl.pallas_call(
        _attn_core_kernel,
        grid=(n // seq, n_t),
        in_specs=[pl.BlockSpec(memory_space=pltpu.SMEM),
                  pl.BlockSpec((tq, ATTN_WIDTH), row), pl.BlockSpec((tq, 2 * ATTN_KV_WIDTH), row),
                  pl.BlockSpec((KEY_BLOCK, 2 * ATTN_KV_WIDTH), prev), pl.BlockSpec((KEY_BLOCK, 2 * ATTN_KV_WIDTH), nxt),
                  pl.BlockSpec((tq, ATTN_WIDTH), row), pl.BlockSpec((tq, D_MODEL), row),
                  pl.BlockSpec((ATTN_WIDTH, D_MODEL), const)],
        out_specs=pl.BlockSpec((tq, D_MODEL), row),
        out_shape=jax.ShapeDtypeStruct((n, D_MODEL), F32),
        scratch_shapes=[pltpu.VMEM((ATTN_KV_HEADS, nblk + 2, 2, KEY_BLOCK, LANES), BF16),
                        pltpu.VMEM((ATTN_KV_HEADS, nblk + 2, 2, KEY_BLOCK, 2 * LANES), BF16),
                        pltpu.VMEM((tq, ATTN_WIDTH), F32)],
        compiler_params=_params(2),
        name="attn_core",
    )(sink, q, kv, kv, kv, gate, x, wo)


ROW_SLABS = 6


def _log_sigmoid(x):
    return jnp.minimum(x, 0.0) - jnp.log1p(jnp.exp(-jnp.abs(x)))


def _segment_scan(x, op, fill, pos_in_chunk, reverse):
    width = x.shape[1]
    d = 1
    while d < CHUNK:
        if reverse:
            shifted = pltpu.roll(x, width - d, 1)
            valid = pos_in_chunk < CHUNK - d
        else:
            shifted = pltpu.roll(x, d, 1)
            valid = pos_in_chunk >= d
        x = op(x, jnp.where(valid, shifted, fill))
        d *= 2
    return x


def _mlstm_project(x_ref, g_ref, w_ref, wg_ref, bias_ref, q_ref, kt_ref, v_ref, og_ref, colg_ref, rowg_ref, xn_scr):
    tm = x_ref.shape[0]
    xn_scr[...] = _normed_bf16(x_ref, g_ref)

    def proj(col, width):
        return jnp.dot(xn_scr[...], w_ref[:, col:col + width], preferred_element_type=F32)

    gates = jnp.dot(xn_scr[...], wg_ref[...], preferred_element_type=F32) + bias_ref[...]
    gt = gates.T
    s = GATE_SLAB
    ig_f, fg_f, ig_b, fg_b = gt[0:s], gt[s:2 * s], gt[2 * s:3 * s], gt[3 * s:4 * s]
    pos = lax.broadcasted_iota(jnp.int32, (s, tm), 1) % CHUNK
    b_f = _segment_scan(_log_sigmoid(fg_f) * LOG2E, jnp.add, 0.0, pos, False)
    u_f = ig_f * LOG2E - b_f
    cu_f = _segment_scan(u_f, jnp.maximum, -jnp.inf, pos, False)
    b_b = _segment_scan(_log_sigmoid(fg_b) * LOG2E, jnp.add, 0.0, pos, True)
    u_b = ig_b * LOG2E - b_b
    cu_b = _segment_scan(u_b, jnp.maximum, -jnp.inf, pos, True)
    pad = jnp.zeros((LANES - 4 * s, tm), F32)
    colg_ref[...] = jnp.concatenate([-b_f, cu_f, -b_b, cu_b, pad], axis=0).T
    rowg_ref[...] = jnp.concatenate([u_f, u_b, b_f, cu_f, b_b, cu_b], axis=0)

    for c in range(MLSTM_QK_WIDTH // 256):
        q_ref[:, 256 * c:256 * (c + 1)] = proj(256 * c, 256).astype(BF16)
    kscale = MLSTM_QK_DIM ** -0.5
    for c in range(MLSTM_QK_WIDTH // 256):
        kt_ref[256 * c:256 * (c + 1), :] = (proj(MLSTM_QK_WIDTH + 256 * c, 256) * kscale).T.astype(BF16)
    v0 = 2 * MLSTM_QK_WIDTH
    for c in range(MLSTM_WIDTH // 256):
        v_ref[:, 256 * c:256 * (c + 1)] = proj(v0 + 256 * c, 256).astype(BF16)
    o0 = v0 + MLSTM_WIDTH
    z0 = o0 + MLSTM_WIDTH

    def gate_chunk(c):
        o = proj(o0 + 256 * c, 256)
        z = proj(z0 + 256 * c, 256)
        og_ref[:, 256 * c:256 * (c + 1)] = (jax.nn.sigmoid(o) * (z * jax.nn.sigmoid(z))).astype(BF16)

    return [functools.partial(gate_chunk, c) for c in range(MLSTM_WIDTH // 256)]


def _mlstm_sweep_tile(q_ref, kt_ref, v_ref, colg_ref, rowg_ref, c_scr, n_scr, m_scr, h_scr, vc_scr,
                      reverse, chunk_done=None, fillers=()):
    assert CHUNK == LANES
    tt = q_ref.shape[0]
    nch = tt // CHUNK
    L = CHUNK
    ti = lax.broadcasted_iota(jnp.int32, (L, L), 0)
    si = lax.broadcasted_iota(jnp.int32, (L, L), 1)
    causal = (si >= ti) if reverse else (si <= ti)
    col0 = 2 * GATE_SLAB if reverse else 0
    urow0 = GATE_SLAB if reverse else 0
    half = MLSTM_V_DIM // 2
    for slot in range(nch * MLSTM_HEADS):
        vc_scr[slot, 0:L, MLSTM_V_DIM:] = jnp.ones((L, LANES), BF16)

    chunks = range(nch - 1, -1, -1) if reverse else range(nch)

    brow0 = (4 if reverse else 2) * GATE_SLAB
    m_prev, m_ref = {}, {}
    m = m_scr[...]
    for c in chunks:
        end = L * c if reverse else L * c + L - 1
        b_end = jnp.broadcast_to(rowg_ref[brow0:brow0 + GATE_SLAB, end:end + 1], (GATE_SLAB, LANES))
        cu_end = jnp.broadcast_to(rowg_ref[brow0 + GATE_SLAB:brow0 + 2 * GATE_SLAB, end:end + 1], (GATE_SLAB, LANES))
        m_new_minus_b = jnp.maximum(m, cu_end)
        for h in range(MLSTM_HEADS):
            m_prev[c, h] = m[h:h + 1, :]
            m_ref[c, h] = m_new_minus_b[h:h + 1, :]
        m = m_new_minus_b + b_end
    m_scr[...] = m

    fillers = list(fillers)
    lhs, floor, dr = {}, {}, {}

    def score_phase(c):
        r0 = L * c
        cg = colg_ref[r0:r0 + L, :]
        if fillers:
            fillers.pop(0)()
        for h in range(MLSTM_HEADS):
            slot = c * MLSTM_HEADS + h
            qh = q_ref[r0:r0 + L, MLSTM_QK_DIM * h:MLSTM_QK_DIM * (h + 1)]
            kth = kt_ref[MLSTM_QK_DIM * h:MLSTM_QK_DIM * (h + 1), r0:r0 + L]
            nb_rep = jnp.broadcast_to(cg[:, col0 + h:col0 + h + 1], (L, LANES))
            cu_rep = jnp.broadcast_to(cg[:, col0 + GATE_SLAB + h:col0 + GATE_SLAB + h + 1], (L, LANES))
            u_r = rowg_ref[urow0 + h:urow0 + h + 1, r0:r0 + L]
            mx = jnp.maximum(cu_rep, m_prev[c, h])
            e = jnp.exp2(jnp.where(causal, u_r - mx, NEG))
            s = jnp.dot(qh, kth, preferred_element_type=F32)
            sc = jnp.exp2(m_prev[c, h] - mx)
            lhs[c, h] = jnp.concatenate([(e * s).astype(BF16), qh * sc.astype(BF16)], axis=1)
            floor[c, h] = jnp.exp2(nb_rep - mx)
            vc_scr[slot, 0:L, 0:MLSTM_V_DIM] = v_ref[r0:r0 + L, MLSTM_V_DIM * h:MLSTM_V_DIM * (h + 1)]

    order = list(chunks)
    for c in order[:SCORE_LOOKAHEAD]:
        score_phase(c)

    pending = None
    for i, c in enumerate(order):
        r0 = L * c
        if i + SCORE_LOOKAHEAD < len(order):
            score_phase(order[i + SCORE_LOOKAHEAD])
        for h in range(MLSTM_HEADS):
            slot = c * MLSTM_HEADS + h
            kth = kt_ref[MLSTM_QK_DIM * h:MLSTM_QK_DIM * (h + 1), r0:r0 + L]
            u_r = rowg_ref[urow0 + h:urow0 + h + 1, r0:r0 + L]
            kwt = kth * jnp.exp2(u_r - m_ref[c, h]).astype(BF16)
            dr[c, h] = jnp.dot(kwt, vc_scr[slot, 0:L, :], preferred_element_type=F32)
        for h in range(MLSTM_HEADS):
            slot = c * MLSTM_HEADS + h
            vc_scr[slot, L:, 0:MLSTM_V_DIM] = c_scr[h].astype(BF16)
            vc_scr[slot, L:, MLSTM_V_DIM:] = n_scr[h].astype(BF16)
            r = jnp.dot(lhs[c, h], vc_scr[slot], preferred_element_type=F32)
            inv = 1.0 / jnp.maximum(jnp.abs(r[:, MLSTM_V_DIM:]), floor[c, h])
            c0 = MLSTM_V_DIM * h
            h_scr[r0:r0 + L, c0:c0 + half] = r[:, :half] * inv
            h_scr[r0:r0 + L, c0 + half:c0 + 2 * half] = r[:, half:2 * half] * inv
            sp = jnp.exp2(m_prev[c, h] - m_ref[c, h])
            c_scr[h] = jnp.concatenate([sp, sp], axis=1) * c_scr[h] + dr[c, h][:, :MLSTM_V_DIM]
            n_scr[h] = sp * n_scr[h] + dr[c, h][:, MLSTM_V_DIM:]
        if chunk_done is not None:
            if pending is not None:
                chunk_done(pending)
            pending = r0
    if pending is not None:
        chunk_done(pending)
    for filler in fillers:
        filler()


def _reset_state(c_scr, n_scr, m_scr):
    c_scr[...] = jnp.zeros(c_scr.shape, F32)
    n_scr[...] = jnp.zeros(n_scr.shape, F32)
    m_scr[...] = jnp.full(m_scr.shape, NEG_INIT, F32)


def _mlstm_in_bwd_kernel(x_ref, g_ref, w_ref, wg_ref, bias_ref, q_ref, kt_ref, v_ref, og_ref, colg_ref, rowg_ref,
                         hb_ref, xn_scr, c_scr, n_scr, m_scr, h_scr, vc_scr):
    @pl.when(pl.program_id(1) == 0)
    def _():
        _reset_state(c_scr, n_scr, m_scr)

    gate_chunks = _mlstm_project(x_ref, g_ref, w_ref, wg_ref, bias_ref, q_ref, kt_ref, v_ref, og_ref, colg_ref,
                                 rowg_ref, xn_scr)
    _mlstm_sweep_tile(q_ref, kt_ref, v_ref, colg_ref, rowg_ref, c_scr, n_scr, m_scr, h_scr, vc_scr, reverse=True,
                      fillers=gate_chunks)
    hb_ref[...] = h_scr[...].astype(BF16)


def _mlstm_fwd_kernel(q_ref, kt_ref, v_ref, colg_ref, rowg_ref, hb_ref, og_ref, hn_ref, x_ref, wo_ref,
                      fg_ref, out_ref, c_scr, n_scr, m_scr, h_scr, vc_scr, *, final_norm):
    @pl.when(pl.program_id(1) == 0)
    def _():
        _reset_state(c_scr, n_scr, m_scr)

    def finish_rows(r0):
        rows = slice(r0, r0 + CHUNK)
        for h in range(MLSTM_HEADS):
            cs = slice(MLSTM_V_DIM * h, MLSTM_V_DIM * (h + 1))
            hh = h_scr[rows, cs] + hb_ref[rows, cs].astype(F32)
            ms = jnp.mean(hh * hh, axis=-1, keepdims=True)
            hh = hh * lax.rsqrt(ms + EPS) * hn_ref[:, cs]
            h_scr[rows, cs] = hh * og_ref[rows, cs].astype(F32)
        y = x_ref[rows, :] + jnp.dot(h_scr[rows, :].astype(BF16), wo_ref[...], preferred_element_type=F32)
        if final_norm:
            ms = jnp.mean(y * y, axis=-1, keepdims=True)
            y = y * lax.rsqrt(ms + EPS) * fg_ref[...]
        out_ref[rows, :] = y

    _mlstm_sweep_tile(q_ref, kt_ref, v_ref, colg_ref, rowg_ref, c_scr, n_scr, m_scr, h_scr, vc_scr, reverse=False,
                      chunk_done=finish_rows)


def _mlstm_state_scratch(tt):
    return [pltpu.VMEM((MLSTM_HEADS, MLSTM_QK_DIM, MLSTM_V_DIM), F32),
            pltpu.VMEM((MLSTM_HEADS, MLSTM_QK_DIM, LANES), F32),
            pltpu.VMEM((GATE_SLAB, LANES), F32),
            pltpu.VMEM((tt, MLSTM_WIDTH), F32),
            pltpu.VMEM((tt // CHUNK * MLSTM_HEADS, CHUNK + MLSTM_QK_DIM, MLSTM_V_DIM + LANES), BF16)]


def _mlstm_in_bwd(x, g, w, wg, bias, seq):
    n = x.shape[0]
    tt = _tile(MLSTM_TILE, seq)
    n_t = seq // tt
    const = lambda b, t: (0, 0)
    row = lambda b, t: (b * n_t + n_t - 1 - t, 0)
    col = lambda b, t: (0, b * n_t + n_t - 1 - t)
    return pl.pallas_call(
        _mlstm_in_bwd_kernel,
        grid=(n // seq, n_t),
        in_specs=[pl.BlockSpec((tt, D_MODEL), row), pl.BlockSpec((1, D_MODEL), const),
                  pl.BlockSpec(w.shape, const, pipeline_mode=pl.Buffered(1)), pl.BlockSpec(wg.shape, const),
                  pl.BlockSpec((1, LANES), const)],
        out_specs=[pl.BlockSpec((tt, MLSTM_QK_WIDTH), row), pl.BlockSpec((MLSTM_QK_WIDTH, tt), col),
                   pl.BlockSpec((tt, MLSTM_WIDTH), row), pl.BlockSpec((tt, MLSTM_WIDTH), row),
                   pl.BlockSpec((tt, LANES), row), pl.BlockSpec((ROW_SLABS * GATE_SLAB, tt), col),
                   pl.BlockSpec((tt, MLSTM_WIDTH), row)],
        out_shape=[jax.ShapeDtypeStruct((n, MLSTM_QK_WIDTH), BF16), jax.ShapeDtypeStruct((MLSTM_QK_WIDTH, n), BF16),
                   jax.ShapeDtypeStruct((n, MLSTM_WIDTH), BF16), jax.ShapeDtypeStruct((n, MLSTM_WIDTH), BF16),
                   jax.ShapeDtypeStruct((n, LANES), F32), jax.ShapeDtypeStruct((ROW_SLABS * GATE_SLAB, n), F32),
                   jax.ShapeDtypeStruct((n, MLSTM_WIDTH), BF16)],
        scratch_shapes=[pltpu.VMEM((tt, D_MODEL), BF16)] + _mlstm_state_scratch(tt),
        compiler_params=_params(2),
        name="mlstm_in_bwd",
    )(x, g, w, wg, bias)


def _mlstm_fwd(q, kt, v, colg, rowg, hb, og, hn, x, wo, fg, seq, final_norm):
    n = x.shape[0]
    tt = _tile(MLSTM_TILE, seq)
    n_t = seq // tt
    const = lambda b, t: (0, 0)
    row = lambda b, t: (b * n_t + t, 0)
    col = lambda b, t: (0, b * n_t + t)
    return pl.pallas_call(
        functools.partial(_mlstm_fwd_kernel, final_norm=final_norm),
        grid=(n // seq, n_t),
        in_specs=[pl.BlockSpec((tt, MLSTM_QK_WIDTH), row), pl.BlockSpec((MLSTM_QK_WIDTH, tt), col),
                  pl.BlockSpec((tt, MLSTM_WIDTH), row), pl.BlockSpec((tt, LANES), row),
                  pl.BlockSpec((ROW_SLABS * GATE_SLAB, tt), col),
                  pl.BlockSpec((tt, MLSTM_WIDTH), row), pl.BlockSpec((tt, MLSTM_WIDTH), row),
                  pl.BlockSpec((1, MLSTM_WIDTH), const), pl.BlockSpec((tt, D_MODEL), row),
                  pl.BlockSpec((MLSTM_WIDTH, D_MODEL), const), pl.BlockSpec((1, D_MODEL), const)],
        out_specs=pl.BlockSpec((tt, D_MODEL), row),
        out_shape=jax.ShapeDtypeStruct((n, D_MODEL), F32),
        scratch_shapes=_mlstm_state_scratch(tt),
        compiler_params=_params(2),
        name="mlstm_fwd",
    )(q, kt, v, colg, rowg, hb, og, hn, x, wo, fg)


def _rope_tables(seq):
    half = ATTN_HEAD_DIM // 2
    inv = jnp.exp(-math.log(ROPE_THETA) * jnp.arange(half, dtype=F32) / half)
    ang = jnp.arange(seq).astype(F32)[:, None] * inv[None, :]
    cos = jnp.cos(ang)
    sin = jnp.sin(ang)
    cos_t = jnp.tile(cos, (1, LANES // half))
    sin_t = jnp.tile(jnp.concatenate([-sin, sin], axis=1), (1, LANES // ATTN_HEAD_DIM))
    return cos_t, sin_t


def _gate_slabs(a):
    lead = a.shape[:-1]
    a = a.reshape(lead + (4, MLSTM_HEADS))
    a = jnp.pad(a, [(0, 0)] * len(lead) + [(0, 0), (0, GATE_SLAB - MLSTM_HEADS)])
    a = a.reshape(lead + (4 * GATE_SLAB,))
    return jnp.pad(a, [(0, 0)] * len(lead) + [(0, LANES - 4 * GATE_SLAB)])


def _trunk(x, p):
    bsz, seq, _ = x.shape
    xf = x.reshape(bsz * seq, D_MODEL)
    cos_t, sin_t = _rope_tables(seq)
    for i in range(4):
        j = i // 2
        g = p["norm_g"][i][None, :]
        if i % 2 == 0:
            q, kv, gate = _attn_in(xf, g, p["attn_w_in"][j], cos_t, sin_t, seq)
            xf = _attn_core(p["attn_sink"][j], q, kv, gate, xf, p["attn_w_out"][j], seq)
        else:
            q, kt, v, og, colg, rowg, hb = _mlstm_in_bwd(xf, g, p["mlstm_w_main"][j], p["mlstm_w_gate"][j],
                                                         p["mlstm_gate_bias"][j], seq)
            xf = _mlstm_fwd(q, kt, v, colg, rowg, hb, og, p["mlstm_head_norm"][j][None, :], xf, p["mlstm_w_out"][j],
                            p["final_norm_g"][None, :], seq, final_norm=(i == 3))
    return xf.reshape(bsz, seq, D_MODEL)


def kernel(x_prompt, x_sample, norm_g, attn_w_in, attn_sink, attn_w_out, mlstm_w_in, mlstm_gate_bias, mlstm_head_norm, mlstm_w_out, final_norm_g):
    p = {
        "norm_g": norm_g.astype(F32),
        "attn_w_in": attn_w_in.astype(BF16),
        "attn_sink": attn_sink.astype(F32),
        "attn_w_out": attn_w_out.astype(BF16),
        "mlstm_w_main": mlstm_w_in[:, :, :MLSTM_MAIN_IN].astype(BF16),
        "mlstm_w_gate": _gate_slabs(mlstm_w_in[:, :, MLSTM_MAIN_IN:]).astype(BF16),
        "mlstm_gate_bias": _gate_slabs(mlstm_gate_bias.astype(F32))[:, None, :],
        "mlstm_head_norm": mlstm_head_norm.astype(F32),
        "mlstm_w_out": mlstm_w_out.astype(BF16),
        "final_norm_g": final_norm_g.astype(F32),
    }
    return _trunk(x_prompt, p), _trunk(x_sample, p)
```

```python
import functools
import math

import jax
import jax.numpy as jnp
from jax import lax
from jax.experimental import pallas as pl
from jax.experimental.pallas import tpu as pltpu

F32 = jnp.float32
BF16 = jnp.bfloat16

D_MODEL = 1024
EPS = 1e-6
NEG = -1e30
LOG2E = math.log2(math.e)

ATTN_HEADS = 16
ATTN_KV_HEADS = 4
ATTN_HEAD_DIM = 64
ATTN_WIDTH = ATTN_HEADS * ATTN_HEAD_DIM
ATTN_KV_WIDTH = ATTN_KV_HEADS * ATTN_HEAD_DIM
WINDOW = 128
ROPE_THETA = 10000.0
KEY_BLOCK = 128

MLSTM_HEADS = 4
MLSTM_V_DIM = 256
MLSTM_QK_DIM = 128
MLSTM_WIDTH = MLSTM_HEADS * MLSTM_V_DIM
MLSTM_QK_WIDTH = MLSTM_HEADS * MLSTM_QK_DIM
MLSTM_MAIN_IN = 2 * MLSTM_QK_WIDTH + 3 * MLSTM_WIDTH
NEG_INIT = -1e30
CHUNK = 128
GATE_SLAB = 8
SCORE_LOOKAHEAD = 2

LANES = 128
PROJ_COLS = 256
VMEM_LIMIT = 56 * 1024 * 1024

PROJ_TILE = 1024
ATTN_TILE = 1024
MLSTM_TILE = 1024


def _params(n_axes):
    return pltpu.CompilerParams(dimension_semantics=("arbitrary",) * n_axes, vmem_limit_bytes=VMEM_LIMIT)


def _tile(preferred, seq):
    tile = min(preferred, seq)
    assert seq % tile == 0 and tile % LANES == 0, (seq, tile)
    return tile


def _normed_bf16(x_ref, g_ref):
    x = x_ref[...]
    ms = jnp.mean(x * x, axis=-1, keepdims=True)
    return (x * lax.rsqrt(ms + EPS) * g_ref[...]).astype(BF16)


def _attn_in_kernel(x_ref, g_ref, w_ref, cos_ref, sin_ref, q_ref, kv_ref, gate_ref, xn_scr):
    tm = x_ref.shape[0]
    xn_scr[...] = _normed_bf16(x_ref, g_ref)
    cos = cos_ref[...]
    sin = sin_ref[...]
    lane = lax.broadcasted_iota(jnp.int32, (tm, LANES), 1)
    first_half = (lane % ATTN_HEAD_DIM) < (ATTN_HEAD_DIM // 2)

    def rope(a):
        partner = jnp.where(first_half, pltpu.roll(a, LANES - 32, 1), pltpu.roll(a, 32, 1))
        return a * cos + partner * sin

    def proj(col, width):
        return jnp.dot(xn_scr[...], w_ref[:, col:col + width], preferred_element_type=F32)

    scale = ATTN_HEAD_DIM ** -0.5 * LOG2E
    for c in range(ATTN_WIDTH // PROJ_COLS):
        acc = proj(PROJ_COLS * c, PROJ_COLS)
        for j in range(2):
            q_ref[:, PROJ_COLS * c + LANES * j:PROJ_COLS * c + LANES * (j + 1)] = (
                rope(acc[:, LANES * j:LANES * (j + 1)]) * scale).astype(BF16)
    acc = proj(ATTN_WIDTH, 2 * ATTN_KV_WIDTH)
    for j in range(ATTN_KV_WIDTH // LANES):
        kv_ref[:, LANES * j:LANES * (j + 1)] = rope(acc[:, LANES * j:LANES * (j + 1)]).astype(BF16)
    kv_ref[:, ATTN_KV_WIDTH:] = acc[:, ATTN_KV_WIDTH:].astype(BF16)
    z0 = ATTN_WIDTH + 2 * ATTN_KV_WIDTH
    for c in range(ATTN_WIDTH // PROJ_COLS):
        z = proj(z0 + PROJ_COLS * c, PROJ_COLS)
        gate_ref[:, PROJ_COLS * c:PROJ_COLS * (c + 1)] = (z * jax.nn.sigmoid(z)).astype(BF16)


def _attn_in(x, g, w, cos_t, sin_t, seq):
    n = x.shape[0]
    tm = _tile(PROJ_TILE, seq)
    tiles_per_seq = seq // tm
    row = lambda i: (i, 0)
    const = lambda i: (0, 0)
    pos = lambda i: (i % tiles_per_seq, 0)
    return pl.pallas_call(
        _attn_in_kernel,
        grid=(n // tm,),
        in_specs=[pl.BlockSpec((tm, D_MODEL), row), pl.BlockSpec((1, D_MODEL), const),
                  pl.BlockSpec(w.shape, const, pipeline_mode=pl.Buffered(1)),
                  pl.BlockSpec((tm, LANES), pos), pl.BlockSpec((tm, LANES), pos)],
        out_specs=[pl.BlockSpec((tm, ATTN_WIDTH), row), pl.BlockSpec((tm, 2 * ATTN_KV_WIDTH), row),
                   pl.BlockSpec((tm, ATTN_WIDTH), row)],
        out_shape=[jax.ShapeDtypeStruct((n, ATTN_WIDTH), BF16), jax.ShapeDtypeStruct((n, 2 * ATTN_KV_WIDTH), BF16),
                   jax.ShapeDtypeStruct((n, ATTN_WIDTH), BF16)],
        scratch_shapes=[pltpu.VMEM((tm, D_MODEL), BF16)],
        compiler_params=_params(1),
        name="attn_in",
    )(x, g, w, cos_t, sin_t)


def _attn_core_kernel(sink_ref, q_ref, kv_ref, kvp_ref, kvn_ref, gate_ref, x_ref, wo_ref, out_ref,
                      kx_scr, vx_scr, o_scr):
    t = pl.program_id(1)
    n_t = pl.num_programs(1)
    tq = q_ref.shape[0]
    nblk = tq // KEY_BLOCK
    kb = KEY_BLOCK
    half = ATTN_HEAD_DIM

    lane = lax.broadcasted_iota(jnp.int32, (kb, LANES), 1)
    low = lane < half
    ones_lo = jnp.where(low, 1.0, 0.0).astype(BF16)
    ones_hi = jnp.where(low, 0.0, 1.0).astype(BF16)

    def expand(blk_ref, r0, jb):
        for vi in range(ATTN_KV_WIDTH // LANES):
            kcol = blk_ref[r0:r0 + kb, LANES * vi:LANES * (vi + 1)].astype(F32)
            vcol = blk_ref[r0:r0 + kb, ATTN_KV_WIDTH + LANES * vi:ATTN_KV_WIDTH + LANES * (vi + 1)].astype(F32)
            kswap = pltpu.roll(kcol, half, 1)
            vswap = pltpu.roll(vcol, half, 1)
            for hf in range(2):
                g = 2 * vi + hf
                k_src_lo, k_src_hi = (kcol, kswap) if hf == 0 else (kswap, kcol)
                v_src_lo, v_src_hi = (vcol, vswap) if hf == 0 else (vswap, vcol)
                kx_scr[g, jb, 0] = jnp.where(low, k_src_lo, 0.0).astype(BF16)
                kx_scr[g, jb, 1] = jnp.where(low, 0.0, k_src_hi).astype(BF16)
                vx_scr[g, jb, 0, :, :LANES] = jnp.where(low, v_src_lo, 0.0).astype(BF16)
                vx_scr[g, jb, 0, :, LANES:] = ones_lo
                vx_scr[g, jb, 1, :, :LANES] = jnp.where(low, 0.0, v_src_hi).astype(BF16)
                vx_scr[g, jb, 1, :, LANES:] = ones_hi

    expand(kvp_ref, 0, 0)
    for jb in range(nblk):
        expand(kv_ref, kb * jb, jb + 1)
    expand(kvn_ref, 0, nblk + 1)

    qi = lax.broadcasted_iota(jnp.int32, (kb, kb), 0)
    ki = lax.broadcasted_iota(jnp.int32, (kb, kb), 1)
    band_prev = jnp.where(ki >= qi, 0.0, NEG).astype(F32)
    band_next = jnp.where(ki <= qi, 0.0, NEG).astype(F32)

    for blk in range(nblk):
        r0 = kb * blk
        bias_prev = jnp.where(t == 0, NEG, band_prev) if blk == 0 else band_prev
        bias_next = jnp.where(t == n_t - 1, NEG, band_next) if blk == nblk - 1 else band_next
        for g in range(ATTN_KV_HEADS):
            c0 = 2 * LANES * g
            q2 = jnp.concatenate([q_ref[r0:r0 + kb, c0:c0 + LANES],
                                  q_ref[r0:r0 + kb, c0 + LANES:c0 + 2 * LANES]], axis=0)
            kx = kx_scr[g, blk:blk + 3].reshape(6 * kb, LANES)
            s = lax.dot_general(q2, kx, (((1,), (1,)), ((), ())), preferred_element_type=F32)
            p_rows = []
            sink_terms = []
            for p in range(2):
                cols = [None] * 6
                maxes = []
                sinks = []
                for ab in range(2):
                    head = 4 * g + 2 * p + ab
                    sink = sink_ref[head] * LOG2E
                    sp = s[kb * p:kb * (p + 1), kb * ab:kb * (ab + 1)] + bias_prev
                    so = s[kb * p:kb * (p + 1), kb * (2 + ab):kb * (3 + ab)]
                    sn = s[kb * p:kb * (p + 1), kb * (4 + ab):kb * (5 + ab)] + bias_next
                    m = jnp.max(jnp.maximum(jnp.maximum(sp, so), sn), axis=1, keepdims=True)
                    m = jnp.maximum(m, sink)
                    cols[ab] = jnp.exp2(sp - m).astype(BF16)
                    cols[2 + ab] = jnp.exp2(so - m).astype(BF16)
                    cols[4 + ab] = jnp.exp2(sn - m).astype(BF16)
                    maxes.append(m)
                    sinks.append(sink)
                p_rows.append(jnp.concatenate(cols, axis=1))
                sink_terms.append(jnp.exp2(jnp.where(low, sinks[0], sinks[1]) - jnp.where(low, maxes[0], maxes[1])))
            pmat = jnp.concatenate(p_rows, axis=0)
            vx = vx_scr[g, blk:blk + 3].reshape(6 * kb, 2 * LANES)
            r = jnp.dot(pmat, vx, preferred_element_type=F32)
            for p in range(2):
                num = r[kb * p:kb * (p + 1), :LANES]
                den = r[kb * p:kb * (p + 1), LANES:] + sink_terms[p]
                o_scr[r0:r0 + kb, c0 + LANES * p:c0 + LANES * (p + 1)] = num / den

    y = (o_scr[...] * gate_ref[...].astype(F32)).astype(BF16)
    out_ref[...] = x_ref[...] + jnp.dot(y, wo_ref[...], preferred_element_type=F32)


def _attn_core(sink, q, kv, gate, x, wo, seq):
    n = x.shape[0]
    tq = _tile(ATTN_TILE, seq)
    n_t = seq // tq
    nblk = tq // KEY_BLOCK
    last_blk = n // KEY_BLOCK - 1
    row = lambda b, t: (b * n_t + t, 0)
    const = lambda b, t: (0, 0)
    prev = lambda b, t: (jnp.maximum((b * n_t + t) * nblk - 1, 0), 0)
    nxt = lambda b, t: (jnp.minimum((b * n_t + t + 1) * nblk, last_blk), 0)
    return pl.pallas_call(
        _attn_core_kernel,
        grid=(n // seq, n_t),
        in_specs=[pl.BlockSpec(memory_space=pltpu.SMEM),
                  pl.BlockSpec((tq, ATTN_WIDTH), row), pl.BlockSpec((tq, 2 * ATTN_KV_WIDTH), row),
                  pl.BlockSpec((KEY_BLOCK, 2 * ATTN_KV_WIDTH), prev), pl.BlockSpec((KEY_BLOCK, 2 * ATTN_KV_WIDTH), nxt),
                  pl.BlockSpec((tq, ATTN_WIDTH), row), pl.BlockSpec((tq, D_MODEL), row),
                  pl.BlockSpec((ATTN_WIDTH, D_MODEL), const)],
        out_specs=pl.BlockSpec((tq, D_MODEL), row),
        out_shape=jax.ShapeDtypeStruct((n, D_MODEL), F32),
        scratch_shapes=[pltpu.VMEM((ATTN_KV_HEADS, nblk + 2, 2, KEY_BLOCK, LANES), BF16),
                        pltpu.VMEM((ATTN_KV_HEADS, nblk + 2, 2, KEY_BLOCK, 2 * LANES), BF16),
                        pltpu.VMEM((tq, ATTN_WIDTH), F32)],
        compiler_params=_params(2),
        name="attn_core",
    )(sink, q, kv, kv, kv, gate, x, wo)


ROW_SLABS = 6


def _log_sigmoid(x):
    return jnp.minimum(x, 0.0) - jnp.log1p(jnp.exp(-jnp.abs(x)))


def _segment_scan(x, op, fill, pos_in_chunk, reverse):
    width = x.shape[1]
    d = 1
    while d < CHUNK:
        if reverse:
            shifted = pltpu.roll(x, width - d, 1)
            valid = pos_in_chunk < CHUNK - d
        else:
            shifted = pltpu.roll(x, d, 1)
            valid = pos_in_chunk >= d
        x = op(x, jnp.where(valid, shifted, fill))
        d *= 2
    return x


def _mlstm_project(x_ref, g_ref, w_ref, wg_ref, bias_ref, q_ref, kt_ref, v_ref, og_ref, colg_ref, rowg_ref, xn_scr):
    tm = x_ref.shape[0]
    xn_scr[...] = _normed_bf16(x_ref, g_ref)

    def proj(col, width):
        return jnp.dot(xn_scr[...], w_ref[:, col:col + width], preferred_element_type=F32)

    gates = jnp.dot(xn_scr[...], wg_ref[...], preferred_element_type=F32) + bias_ref[...]
    gt = gates.T
    s = GATE_SLAB
    ig_f, fg_f, ig_b, fg_b = gt[0:s], gt[s:2 * s], gt[2 * s:3 * s], gt[3 * s:4 * s]
    pos = lax.broadcasted_iota(jnp.int32, (s, tm), 1) % CHUNK
    b_f = _segment_scan(_log_sigmoid(fg_f) * LOG2E, jnp.add, 0.0, pos, False)
    u_f = ig_f * LOG2E - b_f
    cu_f = _segment_scan(u_f, jnp.maximum, -jnp.inf, pos, False)
    b_b = _segment_scan(_log_sigmoid(fg_b) * LOG2E, jnp.add, 0.0, pos, True)
    u_b = ig_b * LOG2E - b_b
    cu_b = _segment_scan(u_b, jnp.maximum, -jnp.inf, pos, True)
    pad = jnp.zeros((LANES - 4 * s, tm), F32)
    colg_ref[...] = jnp.concatenate([-b_f, cu_f, -b_b, cu_b, pad], axis=0).T
    rowg_ref[...] = jnp.concatenate([u_f, u_b, b_f, cu_f, b_b, cu_b], axis=0)

    for c in range(MLSTM_QK_WIDTH // PROJ_COLS):
        q_ref[:, PROJ_COLS * c:PROJ_COLS * (c + 1)] = proj(PROJ_COLS * c, PROJ_COLS).astype(BF16)
    kscale = MLSTM_QK_DIM ** -0.5
    for c in range(MLSTM_QK_WIDTH // PROJ_COLS):
        k = proj(MLSTM_QK_WIDTH + PROJ_COLS * c, PROJ_COLS) * kscale
        kt_ref[PROJ_COLS * c:PROJ_COLS * (c + 1), :] = k.T.astype(BF16)
    v0 = 2 * MLSTM_QK_WIDTH
    for c in range(MLSTM_WIDTH // PROJ_COLS):
        v_ref[:, PROJ_COLS * c:PROJ_COLS * (c + 1)] = proj(v0 + PROJ_COLS * c, PROJ_COLS).astype(BF16)
    o0 = v0 + MLSTM_WIDTH
    z0 = o0 + MLSTM_WIDTH

    def gate_chunk(c):
        o = proj(o0 + PROJ_COLS * c, PROJ_COLS)
        z = proj(z0 + PROJ_COLS * c, PROJ_COLS)
        og_ref[:, PROJ_COLS * c:PROJ_COLS * (c + 1)] = (jax.nn.sigmoid(o) * (z * jax.nn.sigmoid(z))).astype(BF16)

    return [functools.partial(gate_chunk, c) for c in range(MLSTM_WIDTH // PROJ_COLS)]


def _mlstm_sweep_tile(q_ref, kt_ref, v_ref, colg_ref, rowg_ref, c_scr, n_scr, m_scr, h_scr, vc_scr,
                      reverse, chunk_done=None, fillers=()):
    assert CHUNK == LANES
    tt = q_ref.shape[0]
    nch = tt // CHUNK
    L = CHUNK
    ti = lax.broadcasted_iota(jnp.int32, (L, L), 0)
    si = lax.broadcasted_iota(jnp.int32, (L, L), 1)
    causal = (si >= ti) if reverse else (si <= ti)
    col0 = 2 * GATE_SLAB if reverse else 0
    urow0 = GATE_SLAB if reverse else 0
    half = MLSTM_V_DIM // 2
    for slot in range(nch * MLSTM_HEADS):
        vc_scr[slot, 0:L, MLSTM_V_DIM:] = jnp.ones((L, LANES), BF16)

    chunks = range(nch - 1, -1, -1) if reverse else range(nch)

    brow0 = (4 if reverse else 2) * GATE_SLAB
    m_prev, m_ref = {}, {}
    m = m_scr[...]
    for c in chunks:
        end = L * c if reverse else L * c + L - 1
        b_end = jnp.broadcast_to(rowg_ref[brow0:brow0 + GATE_SLAB, end:end + 1], (GATE_SLAB, LANES))
        cu_end = jnp.broadcast_to(rowg_ref[brow0 + GATE_SLAB:brow0 + 2 * GATE_SLAB, end:end + 1], (GATE_SLAB, LANES))
        m_new_minus_b = jnp.maximum(m, cu_end)
        for h in range(MLSTM_HEADS):
            m_prev[c, h] = m[h:h + 1, :]
            m_ref[c, h] = m_new_minus_b[h:h + 1, :]
        m = m_new_minus_b + b_end
    m_scr[...] = m

    fillers = list(fillers)
    lhs, floor, dr = {}, {}, {}

    def score_phase(c):
        r0 = L * c
        cg = colg_ref[r0:r0 + L, :]
        if fillers:
            fillers.pop(0)()
        for h in range(MLSTM_HEADS):
            slot = c * MLSTM_HEADS + h
            qh = q_ref[r0:r0 + L, MLSTM_QK_DIM * h:MLSTM_QK_DIM * (h + 1)]
            kth = kt_ref[MLSTM_QK_DIM * h:MLSTM_QK_DIM * (h + 1), r0:r0 + L]
            nb_rep = jnp.broadcast_to(cg[:, col0 + h:col0 + h + 1], (L, LANES))
            cu_rep = jnp.broadcast_to(cg[:, col0 + GATE_SLAB + h:col0 + GATE_SLAB + h + 1], (L, LANES))
            u_r = rowg_ref[urow0 + h:urow0 + h + 1, r0:r0 + L]
            mx = jnp.maximum(cu_rep, m_prev[c, h])
            e = jnp.exp2(jnp.where(causal, u_r - mx, NEG))
            s = jnp.dot(qh, kth, preferred_element_type=F32)
            sc = jnp.exp2(m_prev[c, h] - mx)
            lhs[c, h] = jnp.concatenate([(e * s).astype(BF16), qh * sc.astype(BF16)], axis=1)
            floor[c, h] = jnp.exp2(nb_rep - mx)
            vc_scr[slot, 0:L, 0:MLSTM_V_DIM] = v_ref[r0:r0 + L, MLSTM_V_DIM * h:MLSTM_V_DIM * (h + 1)]

    order = list(chunks)
    for c in order[:SCORE_LOOKAHEAD]:
        score_phase(c)

    pending = None
    for i, c in enumerate(order):
        r0 = L * c
        if i + SCORE_LOOKAHEAD < len(order):
            score_phase(order[i + SCORE_LOOKAHEAD])
        for h in range(MLSTM_HEADS):
            slot = c * MLSTM_HEADS + h
            kth = kt_ref[MLSTM_QK_DIM * h:MLSTM_QK_DIM * (h + 1), r0:r0 + L]
            u_r = rowg_ref[urow0 + h:urow0 + h + 1, r0:r0 + L]
            kwt = kth * jnp.exp2(u_r - m_ref[c, h]).astype(BF16)
            dr[c, h] = jnp.dot(kwt, vc_scr[slot, 0:L, :], preferred_element_type=F32)
        for h in range(MLSTM_HEADS):
            slot = c * MLSTM_HEADS + h
            vc_scr[slot, L:, 0:MLSTM_V_DIM] = c_scr[h].astype(BF16)
            vc_scr[slot, L:, MLSTM_V_DIM:] = n_scr[h].astype(BF16)
            r = jnp.dot(lhs[c, h], vc_scr[slot], preferred_element_type=F32)
            inv = 1.0 / jnp.maximum(jnp.abs(r[:, MLSTM_V_DIM:]), floor[c, h])
            c0 = MLSTM_V_DIM * h
            h_scr[r0:r0 + L, c0:c0 + half] = r[:, :half] * inv
            h_scr[r0:r0 + L, c0 + half:c0 + 2 * half] = r[:, half:2 * half] * inv
            sp = jnp.exp2(m_prev[c, h] - m_ref[c, h])
            c_scr[h] = jnp.concatenate([sp, sp], axis=1) * c_scr[h] + dr[c, h][:, :MLSTM_V_DIM]
            n_scr[h] = sp * n_scr[h] + dr[c, h][:, MLSTM_V_DIM:]
        if chunk_done is not None:
            if pending is not None:
                chunk_done(pending)
            pending = r0
    if pending is not None:
        chunk_done(pending)
    for filler in fillers:
        filler()


def _reset_state(c_scr, n_scr, m_scr):
    c_scr[...] = jnp.zeros(c_scr.shape, F32)
    n_scr[...] = jnp.zeros(n_scr.shape, F32)
    m_scr[...] = jnp.full(m_scr.shape, NEG_INIT, F32)


def _mlstm_in_bwd_kernel(x_ref, g_ref, w_ref, wg_ref, bias_ref, q_ref, kt_ref, v_ref, og_ref, colg_ref, rowg_ref,
                         hb_ref, xn_scr, c_scr, n_scr, m_scr, h_scr, vc_scr):
    @pl.when(pl.program_id(1) == 0)
    def _():
        _reset_state(c_scr, n_scr, m_scr)

    gate_chunks = _mlstm_project(x_ref, g_ref, w_ref, wg_ref, bias_ref, q_ref, kt_ref, v_ref, og_ref, colg_ref,
                                 rowg_ref, xn_scr)
    _mlstm_sweep_tile(q_ref, kt_ref, v_ref, colg_ref, rowg_ref, c_scr, n_scr, m_scr, h_scr, vc_scr, reverse=True,
                      fillers=gate_chunks)
    hb_ref[...] = h_scr[...].astype(BF16)


def _mlstm_fwd_kernel(q_ref, kt_ref, v_ref, colg_ref, rowg_ref, hb_ref, og_ref, hn_ref, x_ref, wo_ref,
                      fg_ref, out_ref, c_scr, n_scr, m_scr, h_scr, vc_scr, *, final_norm):
    @pl.when(pl.program_id(1) == 0)
    def _():
        _reset_state(c_scr, n_scr, m_scr)

    def finish_rows(r0):
        rows = slice(r0, r0 + CHUNK)
        for h in range(MLSTM_HEADS):
            cs = slice(MLSTM_V_DIM * h, MLSTM_V_DIM * (h + 1))
            hh = h_scr[rows, cs] + hb_ref[rows, cs].astype(F32)
            ms = jnp.mean(hh * hh, axis=-1, keepdims=True)
            hh = hh * lax.rsqrt(ms + EPS) * hn_ref[:, cs]
            h_scr[rows, cs] = hh * og_ref[rows, cs].astype(F32)
        y = x_ref[rows, :] + jnp.dot(h_scr[rows, :].astype(BF16), wo_ref[...], preferred_element_type=F32)
        if final_norm:
            ms = jnp.mean(y * y, axis=-1, keepdims=True)
            y = y * lax.rsqrt(ms + EPS) * fg_ref[...]
        out_ref[rows, :] = y

    _mlstm_sweep_tile(q_ref, kt_ref, v_ref, colg_ref, rowg_ref, c_scr, n_scr, m_scr, h_scr, vc_scr, reverse=False,
                      chunk_done=finish_rows)


def _mlstm_state_scratch(tt):
    return [pltpu.VMEM((MLSTM_HEADS, MLSTM_QK_DIM, MLSTM_V_DIM), F32),
            pltpu.VMEM((MLSTM_HEADS, MLSTM_QK_DIM, LANES), F32),
            pltpu.VMEM((GATE_SLAB, LANES), F32),
            pltpu.VMEM((tt, MLSTM_WIDTH), F32),
            pltpu.VMEM((tt // CHUNK * MLSTM_HEADS, CHUNK + MLSTM_QK_DIM, MLSTM_V_DIM + LANES), BF16)]


def _mlstm_in_bwd(x, g, w, wg, bias, seq):
    n = x.shape[0]
    tt = _tile(MLSTM_TILE, seq)
    n_t = seq // tt
    const = lambda b, t: (0, 0)
    row = lambda b, t: (b * n_t + n_t - 1 - t, 0)
    col = lambda b, t: (0, b * n_t + n_t - 1 - t)
    return pl.pallas_call(
        _mlstm_in_bwd_kernel,
        grid=(n // seq, n_t),
        in_specs=[pl.BlockSpec((tt, D_MODEL), row), pl.BlockSpec((1, D_MODEL), const),
                  pl.BlockSpec(w.shape, const, pipeline_mode=pl.Buffered(1)), pl.BlockSpec(wg.shape, const),
                  pl.BlockSpec((1, LANES), const)],
        out_specs=[pl.BlockSpec((tt, MLSTM_QK_WIDTH), row), pl.BlockSpec((MLSTM_QK_WIDTH, tt), col),
                   pl.BlockSpec((tt, MLSTM_WIDTH), row), pl.BlockSpec((tt, MLSTM_WIDTH), row),
                   pl.BlockSpec((tt, LANES), row), pl.BlockSpec((ROW_SLABS * GATE_SLAB, tt), col),
                   pl.BlockSpec((tt, MLSTM_WIDTH), row)],
        out_shape=[jax.ShapeDtypeStruct((n, MLSTM_QK_WIDTH), BF16), jax.ShapeDtypeStruct((MLSTM_QK_WIDTH, n), BF16),
                   jax.ShapeDtypeStruct((n, MLSTM_WIDTH), BF16), jax.ShapeDtypeStruct((n, MLSTM_WIDTH), BF16),
                   jax.ShapeDtypeStruct((n, LANES), F32), jax.ShapeDtypeStruct((ROW_SLABS * GATE_SLAB, n), F32),
                   jax.ShapeDtypeStruct((n, MLSTM_WIDTH), BF16)],
        scratch_shapes=[pltpu.VMEM((tt, D_MODEL), BF16)] + _mlstm_state_scratch(tt),
        compiler_params=_params(2),
        name="mlstm_in_bwd",
    )(x, g, w, wg, bias)


def _mlstm_fwd(q, kt, v, colg, rowg, hb, og, hn, x, wo, fg, seq, final_norm):
    n = x.shape[0]
    tt = _tile(MLSTM_TILE, seq)
    n_t = seq // tt
    const = lambda b, t: (0, 0)
    row = lambda b, t: (b * n_t + t, 0)
    col = lambda b, t: (0, b * n_t + t)
    return pl.pallas_call(
        functools.partial(_mlstm_fwd_kernel, final_norm=final_norm),
        grid=(n // seq, n_t),
        in_specs=[pl.BlockSpec((tt, MLSTM_QK_WIDTH), row), pl.BlockSpec((MLSTM_QK_WIDTH, tt), col),
                  pl.BlockSpec((tt, MLSTM_WIDTH), row), pl.BlockSpec((tt, LANES), row),
                  pl.BlockSpec((ROW_SLABS * GATE_SLAB, tt), col),
                  pl.BlockSpec((tt, MLSTM_WIDTH), row), pl.BlockSpec((tt, MLSTM_WIDTH), row),
                  pl.BlockSpec((1, MLSTM_WIDTH), const), pl.BlockSpec((tt, D_MODEL), row),
                  pl.BlockSpec((MLSTM_WIDTH, D_MODEL), const), pl.BlockSpec((1, D_MODEL), const)],
        out_specs=pl.BlockSpec((tt, D_MODEL), row),
        out_shape=jax.ShapeDtypeStruct((n, D_MODEL), F32),
        scratch_shapes=_mlstm_state_scratch(tt),
        compiler_params=_params(2),
        name="mlstm_fwd",
    )(q, kt, v, colg, rowg, hb, og, hn, x, wo, fg)


def _rope_tables(seq):
    half = ATTN_HEAD_DIM // 2
    inv = jnp.exp(-math.log(ROPE_THETA) * jnp.arange(half, dtype=F32) / half)
    ang = jnp.arange(seq).astype(F32)[:, None] * inv[None, :]
    cos = jnp.cos(ang)
    sin = jnp.sin(ang)
    cos_t = jnp.tile(cos, (1, LANES // half))
    sin_t = jnp.tile(jnp.concatenate([-sin, sin], axis=1), (1, LANES // ATTN_HEAD_DIM))
    return cos_t, sin_t


def _gate_slabs(a):
    lead = a.shape[:-1]
    a = a.reshape(lead + (4, MLSTM_HEADS))
    a = jnp.pad(a, [(0, 0)] * len(lead) + [(0, 0), (0, GATE_SLAB - MLSTM_HEADS)])
    a = a.reshape(lead + (4 * GATE_SLAB,))
    return jnp.pad(a, [(0, 0)] * len(lead) + [(0, LANES - 4 * GATE_SLAB)])


def _trunk(x, p):
    bsz, seq, _ = x.shape
    xf = x.reshape(bsz * seq, D_MODEL)
    cos_t, sin_t = _rope_tables(seq)
    for i in range(4):
        j = i // 2
        g = p["norm_g"][i][None, :]
        if i % 2 == 0:
            q, kv, gate = _attn_in(xf, g, p["attn_w_in"][j], cos_t, sin_t, seq)
            xf = _attn_core(p["attn_sink"][j], q, kv, gate, xf, p["attn_w_out"][j], seq)
        else:
            q, kt, v, og, colg, rowg, hb = _mlstm_in_bwd(xf, g, p["mlstm_w_main"][j], p["mlstm_w_gate"][j],
                                                         p["mlstm_gate_bias"][j], seq)
            xf = _mlstm_fwd(q, kt, v, colg, rowg, hb, og, p["mlstm_head_norm"][j][None, :], xf, p["mlstm_w_out"][j],
                            p["final_norm_g"][None, :], seq, final_norm=(i == 3))
    return xf.reshape(bsz, seq, D_MODEL)


def kernel(x_prompt, x_sample, norm_g, attn_w_in, attn_sink, attn_w_out, mlstm_w_in, mlstm_gate_bias, mlstm_head_norm, mlstm_w_out, final_norm_g):
    p = {
        "norm_g": norm_g,
        "attn_w_in": attn_w_in.astype(BF16),
        "attn_sink": attn_sink,
        "attn_w_out": attn_w_out.astype(BF16),
        "mlstm_w_main": mlstm_w_in[:, :, :MLSTM_MAIN_IN].astype(BF16),
        "mlstm_w_gate": _gate_slabs(mlstm_w_in[:, :, MLSTM_MAIN_IN:]).astype(BF16),
        "mlstm_gate_bias": _gate_slabs(mlstm_gate_bias)[:, None, :],
        "mlstm_head_norm": mlstm_head_norm,
        "mlstm_w_out": mlstm_w_out.astype(BF16),
        "final_norm_g": final_norm_g,
    }
    return _trunk(x_prompt, p), _trunk(x_sample, p)
```

```python
import functools
import math

import jax
import jax.numpy as jnp
from jax import lax
from jax.experimental import pallas as pl
from jax.experimental.pallas import tpu as pltpu

F32 = jnp.float32
BF16 = jnp.bfloat16

D_MODEL = 1024
EPS = 1e-6
NEG = -1e30
LOG2E = math.log2(math.e)

ATTN_HEADS = 16
ATTN_KV_HEADS = 4
ATTN_HEAD_DIM = 64
ATTN_WIDTH = ATTN_HEADS * ATTN_HEAD_DIM
ATTN_KV_WIDTH = ATTN_KV_HEADS * ATTN_HEAD_DIM
WINDOW = 128
ROPE_THETA = 10000.0
KEY_BLOCK = 128

MLSTM_HEADS = 4
MLSTM_V_DIM = 256
MLSTM_QK_DIM = 128
MLSTM_WIDTH = MLSTM_HEADS * MLSTM_V_DIM
MLSTM_QK_WIDTH = MLSTM_HEADS * MLSTM_QK_DIM
MLSTM_MAIN_IN = 2 * MLSTM_QK_WIDTH + 3 * MLSTM_WIDTH
NEG_INIT = -1e30
CHUNK = 128
GATE_SLAB = 8
SCORE_LOOKAHEAD = 2

LANES = 128
PROJ_COLS = 256
VMEM_LIMIT = 56 * 1024 * 1024

PROJ_TILE = 1024
ATTN_TILE = 1024
MLSTM_TILE = 1024


def _params(n_axes):
    return pltpu.CompilerParams(dimension_semantics=("arbitrary",) * n_axes, vmem_limit_bytes=VMEM_LIMIT)


def _tile(preferred, seq):
    tile = min(preferred, seq)
    assert seq % tile == 0 and tile % LANES == 0, (seq, tile)
    return tile


def _normed_bf16(x_ref, g_ref):
    x = x_ref[...]
    ms = jnp.mean(x * x, axis=-1, keepdims=True)
    return (x * lax.rsqrt(ms + EPS) * g_ref[...]).astype(BF16)


def _attn_in_kernel(x_ref, g_ref, w_ref, cos_ref, sin_ref, q_ref, kv_ref, gate_ref, xn_scr):
    tm = x_ref.shape[0]
    xn_scr[...] = _normed_bf16(x_ref, g_ref)
    cos = cos_ref[...]
    sin = sin_ref[...]
    lane = lax.broadcasted_iota(jnp.int32, (tm, LANES), 1)
    first_half = (lane % ATTN_HEAD_DIM) < (ATTN_HEAD_DIM // 2)

    def rope(a):
        partner = jnp.where(first_half, pltpu.roll(a, LANES - 32, 1), pltpu.roll(a, 32, 1))
        return a * cos + partner * sin

    def proj(col, width):
        return jnp.dot(xn_scr[...], w_ref[:, col:col + width], preferred_element_type=F32)

    scale = ATTN_HEAD_DIM ** -0.5 * LOG2E
    for c in range(ATTN_WIDTH // PROJ_COLS):
        acc = proj(PROJ_COLS * c, PROJ_COLS)
        for j in range(2):
            q_ref[:, PROJ_COLS * c + LANES * j:PROJ_COLS * c + LANES * (j + 1)] = (
                rope(acc[:, LANES * j:LANES * (j + 1)]) * scale).astype(BF16)
    acc = proj(ATTN_WIDTH, 2 * ATTN_KV_WIDTH)
    for j in range(ATTN_KV_WIDTH // LANES):
        kv_ref[:, LANES * j:LANES * (j + 1)] = rope(acc[:, LANES * j:LANES * (j + 1)]).astype(BF16)
    kv_ref[:, ATTN_KV_WIDTH:] = acc[:, ATTN_KV_WIDTH:].astype(BF16)
    z0 = ATTN_WIDTH + 2 * ATTN_KV_WIDTH
    for c in range(ATTN_WIDTH // PROJ_COLS):
        z = proj(z0 + PROJ_COLS * c, PROJ_COLS)
        gate_ref[:, PROJ_COLS * c:PROJ_COLS * (c + 1)] = (z * jax.nn.sigmoid(z)).astype(BF16)


def _attn_in(x, g, w, cos_t, sin_t, seq):
    n = x.shape[0]
    tm = _tile(PROJ_TILE, seq)
    tiles_per_seq = seq // tm
    row = lambda i: (i, 0)
    const = lambda i: (0, 0)
    pos = lambda i: (i % tiles_per_seq, 0)
    return pl.pallas_call(
        _attn_in_kernel,
        grid=(n // tm,),
        in_specs=[pl.BlockSpec((tm, D_MODEL), row), pl.BlockSpec((1, D_MODEL), const),
                  pl.BlockSpec(w.shape, const, pipeline_mode=pl.Buffered(1)),
                  pl.BlockSpec((tm, LANES), pos), pl.BlockSpec((tm, LANES), pos)],
        out_specs=[pl.BlockSpec((tm, ATTN_WIDTH), row), pl.BlockSpec((tm, 2 * ATTN_KV_WIDTH), row),
                   pl.BlockSpec((tm, ATTN_WIDTH), row)],
        out_shape=[jax.ShapeDtypeStruct((n, ATTN_WIDTH), BF16), jax.ShapeDtypeStruct((n, 2 * ATTN_KV_WIDTH), BF16),
                   jax.ShapeDtypeStruct((n, ATTN_WIDTH), BF16)],
        scratch_shapes=[pltpu.VMEM((tm, D_MODEL), BF16)],
        compiler_params=_params(1),
        name="attn_in",
    )(x, g, w, cos_t, sin_t)


def _attn_core_kernel(sink_ref, q_ref, kv_ref, kvp_ref, kvn_ref, gate_ref, x_ref, wo_ref, out_ref,
                      kx_scr, vx_scr, o_scr):
    t = pl.program_id(1)
    n_t = pl.num_programs(1)
    tq = q_ref.shape[0]
    nblk = tq // KEY_BLOCK
    kb = KEY_BLOCK
    half = ATTN_HEAD_DIM

    lane = lax.broadcasted_iota(jnp.int32, (kb, LANES), 1)
    low = lane < half
    ones_lo = jnp.where(low, 1.0, 0.0).astype(BF16)
    ones_hi = jnp.where(low, 0.0, 1.0).astype(BF16)

    def expand(blk_ref, r0, jb):
        for vi in range(ATTN_KV_WIDTH // LANES):
            kcol = blk_ref[r0:r0 + kb, LANES * vi:LANES * (vi + 1)].astype(F32)
            vcol = blk_ref[r0:r0 + kb, ATTN_KV_WIDTH + LANES * vi:ATTN_KV_WIDTH + LANES * (vi + 1)].astype(F32)
            kswap = pltpu.roll(kcol, half, 1)
            vswap = pltpu.roll(vcol, half, 1)
            for hf in range(2):
                g = 2 * vi + hf
                k_src_lo, k_src_hi = (kcol, kswap) if hf == 0 else (kswap, kcol)
                v_src_lo, v_src_hi = (vcol, vswap) if hf == 0 else (vswap, vcol)
                kx_scr[g, jb, 0] = jnp.where(low, k_src_lo, 0.0).astype(BF16)
                kx_scr[g, jb, 1] = jnp.where(low, 0.0, k_src_hi).astype(BF16)
                vx_scr[g, jb, 0, :, :LANES] = jnp.where(low, v_src_lo, 0.0).astype(BF16)
                vx_scr[g, jb, 0, :, LANES:] = ones_lo
                vx_scr[g, jb, 1, :, :LANES] = jnp.where(low, 0.0, v_src_hi).astype(BF16)
                vx_scr[g, jb, 1, :, LANES:] = ones_hi

    expand(kvp_ref, 0, 0)
    for jb in range(nblk):
        expand(kv_ref, kb * jb, jb + 1)
    expand(kvn_ref, 0, nblk + 1)

    qi = lax.broadcasted_iota(jnp.int32, (kb, kb), 0)
    ki = lax.broadcasted_iota(jnp.int32, (kb, kb), 1)
    band_prev = jnp.where(ki >= qi, 0.0, NEG).astype(F32)
    band_next = jnp.where(ki <= qi, 0.0, NEG).astype(F32)

    for blk in range(nblk):
        r0 = kb * blk
        bias_prev = jnp.where(t == 0, NEG, band_prev) if blk == 0 else band_prev
        bias_next = jnp.where(t == n_t - 1, NEG, band_next) if blk == nblk - 1 else band_next
        for g in range(ATTN_KV_HEADS):
            c0 = 2 * LANES * g
            q2 = jnp.concatenate([q_ref[r0:r0 + kb, c0:c0 + LANES],
                                  q_ref[r0:r0 + kb, c0 + LANES:c0 + 2 * LANES]], axis=0)
            kx = kx_scr[g, blk:blk + 3].reshape(6 * kb, LANES)
            s = lax.dot_general(q2, kx, (((1,), (1,)), ((), ())), preferred_element_type=F32)
            p_rows = []
            sink_terms = []
            for p in range(2):
                cols = [None] * 6
                maxes = []
                sinks = []
                for ab in range(2):
                    head = 4 * g + 2 * p + ab
                    sink = sink_ref[head] * LOG2E
                    sp = s[kb * p:kb * (p + 1), kb * ab:kb * (ab + 1)] + bias_prev
                    so = s[kb * p:kb * (p + 1), kb * (2 + ab):kb * (3 + ab)]
                    sn = s[kb * p:kb * (p + 1), kb * (4 + ab):kb * (5 + ab)] + bias_next
                    m = jnp.max(jnp.maximum(jnp.maximum(sp, so), sn), axis=1, keepdims=True)
                    m = jnp.maximum(m, sink)
                    cols[ab] = jnp.exp2(sp - m).astype(BF16)
                    cols[2 + ab] = jnp.exp2(so - m).astype(BF16)
                    cols[4 + ab] = jnp.exp2(sn - m).astype(BF16)
                    maxes.append(m)
                    sinks.append(sink)
                p_rows.append(jnp.concatenate(cols, axis=1))
                sink_terms.append(jnp.exp2(jnp.where(low, sinks[0], sinks[1]) - jnp.where(low, maxes[0], maxes[1])))
            pmat = jnp.concatenate(p_rows, axis=0)
            vx = vx_scr[g, blk:blk + 3].reshape(6 * kb, 2 * LANES)
            r = jnp.dot(pmat, vx, preferred_element_type=F32)
            for p in range(2):
                num = r[kb * p:kb * (p + 1), :LANES]
                den = r[kb * p:kb * (p + 1), LANES:] + sink_terms[p]
                o_scr[r0:r0 + kb, c0 + LANES * p:c0 + LANES * (p + 1)] = num / den

    y = (o_scr[...] * gate_ref[...].astype(F32)).astype(BF16)
    out_ref[...] = x_ref[...] + jnp.dot(y, wo_ref[...], preferred_element_type=F32)


def _attn_core(sink, q, kv, gate, x, wo, seq):
    n = x.shape[0]
    tq = _tile(ATTN_TILE, seq)
    n_t = seq // tq
    nblk = tq // KEY_BLOCK
    last_blk = n // KEY_BLOCK - 1
    row = lambda b, t: (b * n_t + t, 0)
    const = lambda b, t: (0, 0)
    prev = lambda b, t: (jnp.maximum((b * n_t + t) * nblk - 1, 0), 0)
    nxt = lambda b, t: (jnp.minimum((b * n_t + t + 1) * nblk, last_blk), 0)
    return pl.pallas_call(
        _attn_core_kernel,
        grid=(n // seq, n_t),
        in_specs=[pl.BlockSpec(memory_space=pltpu.SMEM),
                  pl.BlockSpec((tq, ATTN_WIDTH), row), pl.BlockSpec((tq, 2 * ATTN_KV_WIDTH), row),
                  pl.BlockSpec((KEY_BLOCK, 2 * ATTN_KV_WIDTH), prev), pl.BlockSpec((KEY_BLOCK, 2 * ATTN_KV_WIDTH), nxt),
                  pl.BlockSpec((tq, ATTN_WIDTH), row), pl.BlockSpec((tq, D_MODEL), row),
                  pl.BlockSpec((ATTN_WIDTH, D_MODEL), const)],
        out_specs=pl.BlockSpec((tq, D_MODEL), row),
        out_shape=jax.ShapeDtypeStruct((n, D_MODEL), F32),
        scratch_shapes=[pltpu.VMEM((ATTN_KV_HEADS, nblk + 2, 2, KEY_BLOCK, LANES), BF16),
                        pltpu.VMEM((ATTN_KV_HEADS, nblk + 2, 2, KEY_BLOCK, 2 * LANES), BF16),
                        pltpu.VMEM((tq, ATTN_WIDTH), F32)],
        compiler_params=_params(2),
        name="attn_core",
    )(sink, q, kv, kv, kv, gate, x, wo)


ROW_SLABS = 6


def _log_sigmoid(x):
    return jnp.minimum(x, 0.0) - jnp.log1p(jnp.exp(-jnp.abs(x)))


def _segment_scan(x, op, fill, pos_in_chunk, reverse):
    width = x.shape[1]
    d = 1
    while d < CHUNK:
        if reverse:
            shifted = pltpu.roll(x, width - d, 1)
            valid = pos_in_chunk < CHUNK - d
        else:
            shifted = pltpu.roll(x, d, 1)
            valid = pos_in_chunk >= d
        x = op(x, jnp.where(valid, shifted, fill))
        d *= 2
    return x


def _mlstm_project(x_ref, g_ref, w_ref, wg_ref, bias_ref, q_ref, kt_ref, v_ref, og_ref, colg_ref, rowg_ref, xn_scr):
    tm = x_ref.shape[0]
    xn_scr[...] = _normed_bf16(x_ref, g_ref)

    def proj(col, width):
        return jnp.dot(xn_scr[...], w_ref[:, col:col + width], preferred_element_type=F32)

    gates = jnp.dot(xn_scr[...], wg_ref[...], preferred_element_type=F32) + bias_ref[...]
    gt = gates.T
    s = GATE_SLAB
    ig_f, fg_f, ig_b, fg_b = gt[0:s], gt[s:2 * s], gt[2 * s:3 * s], gt[3 * s:4 * s]
    pos = lax.broadcasted_iota(jnp.int32, (s, tm), 1) % CHUNK
    b_f = _segment_scan(_log_sigmoid(fg_f) * LOG2E, jnp.add, 0.0, pos, False)
    u_f = ig_f * LOG2E - b_f
    cu_f = _segment_scan(u_f, jnp.maximum, -jnp.inf, pos, False)
    b_b = _segment_scan(_log_sigmoid(fg_b) * LOG2E, jnp.add, 0.0, pos, True)
    u_b = ig_b * LOG2E - b_b
    cu_b = _segment_scan(u_b, jnp.maximum, -jnp.inf, pos, True)
    pad = jnp.zeros((LANES - 4 * s, tm), F32)
    colg_ref[...] = jnp.concatenate([-b_f, cu_f, -b_b, cu_b, pad], axis=0).T
    rowg_ref[...] = jnp.concatenate([u_f, u_b, b_f, cu_f, b_b, cu_b], axis=0)

    for c in range(MLSTM_QK_WIDTH // PROJ_COLS):
        q_ref[:, PROJ_COLS * c:PROJ_COLS * (c + 1)] = proj(PROJ_COLS * c, PROJ_COLS).astype(BF16)
    kscale = MLSTM_QK_DIM ** -0.5
    for c in range(MLSTM_QK_WIDTH // PROJ_COLS):
        k = proj(MLSTM_QK_WIDTH + PROJ_COLS * c, PROJ_COLS) * kscale
        kt_ref[PROJ_COLS * c:PROJ_COLS * (c + 1), :] = k.T.astype(BF16)
    v0 = 2 * MLSTM_QK_WIDTH
    for c in range(MLSTM_WIDTH // PROJ_COLS):
        v_ref[:, PROJ_COLS * c:PROJ_COLS * (c + 1)] = proj(v0 + PROJ_COLS * c, PROJ_COLS).astype(BF16)
    o0 = v0 + MLSTM_WIDTH
    z0 = o0 + MLSTM_WIDTH

    def gate_chunk(c):
        o = proj(o0 + PROJ_COLS * c, PROJ_COLS)
        z = proj(z0 + PROJ_COLS * c, PROJ_COLS)
        og_ref[:, PROJ_COLS * c:PROJ_COLS * (c + 1)] = (jax.nn.sigmoid(o) * (z * jax.nn.sigmoid(z))).astype(BF16)

    return [functools.partial(gate_chunk, c) for c in range(MLSTM_WIDTH // PROJ_COLS)]


def _mlstm_sweep_tile(q_ref, kt_ref, v_ref, colg_ref, rowg_ref, c_scr, n_scr, m_scr, h_scr, vc_scr,
                      reverse, chunk_done=None, fillers=()):
    assert CHUNK == LANES
    tt = q_ref.shape[0]
    nch = tt // CHUNK
    L = CHUNK
    ti = lax.broadcasted_iota(jnp.int32, (L, L), 0)
    si = lax.broadcasted_iota(jnp.int32, (L, L), 1)
    causal = (si >= ti) if reverse else (si <= ti)
    col0 = 2 * GATE_SLAB if reverse else 0
    urow0 = GATE_SLAB if reverse else 0
    half = MLSTM_V_DIM // 2
    for slot in range(nch * MLSTM_HEADS):
        vc_scr[slot, 0:L, MLSTM_V_DIM:] = jnp.ones((L, LANES), BF16)

    chunks = range(nch - 1, -1, -1) if reverse else range(nch)

    brow0 = (4 if reverse else 2) * GATE_SLAB
    m_prev, m_ref = {}, {}
    m = m_scr[...]
    for c in chunks:
        end = L * c if reverse else L * c + L - 1
        b_end = jnp.broadcast_to(rowg_ref[brow0:brow0 + GATE_SLAB, end:end + 1], (GATE_SLAB, LANES))
        cu_end = jnp.broadcast_to(rowg_ref[brow0 + GATE_SLAB:brow0 + 2 * GATE_SLAB, end:end + 1], (GATE_SLAB, LANES))
        m_new_minus_b = jnp.maximum(m, cu_end)
        for h in range(MLSTM_HEADS):
            m_prev[c, h] = m[h:h + 1, :]
            m_ref[c, h] = m_new_minus_b[h:h + 1, :]
        m = m_new_minus_b + b_end
    m_scr[...] = m

    fillers = list(fillers)
    lhs, floor, dr, dn = {}, {}, {}, {}

    def score_phase(c):
        r0 = L * c
        cg = colg_ref[r0:r0 + L, :]
        if fillers:
            fillers.pop(0)()
        for h in range(MLSTM_HEADS):
            slot = c * MLSTM_HEADS + h
            qh = q_ref[r0:r0 + L, MLSTM_QK_DIM * h:MLSTM_QK_DIM * (h + 1)]
            kth = kt_ref[MLSTM_QK_DIM * h:MLSTM_QK_DIM * (h + 1), r0:r0 + L]
            nb_rep = jnp.broadcast_to(cg[:, col0 + h:col0 + h + 1], (L, LANES))
            cu_rep = jnp.broadcast_to(cg[:, col0 + GATE_SLAB + h:col0 + GATE_SLAB + h + 1], (L, LANES))
            u_r = rowg_ref[urow0 + h:urow0 + h + 1, r0:r0 + L]
            mx = jnp.maximum(cu_rep, m_prev[c, h])
            e = jnp.exp2(jnp.where(causal, u_r - mx, NEG))
            s = jnp.dot(qh, kth, preferred_element_type=F32)
            sc = jnp.exp2(m_prev[c, h] - mx)
            lhs[c, h] = jnp.concatenate([(e * s).astype(BF16), qh * sc.astype(BF16)], axis=1)
            floor[c, h] = jnp.exp2(nb_rep - mx)
            vc_scr[slot, 0:L, 0:MLSTM_V_DIM] = v_ref[r0:r0 + L, MLSTM_V_DIM * h:MLSTM_V_DIM * (h + 1)]

    order = list(chunks)
    for c in order[:SCORE_LOOKAHEAD]:
        score_phase(c)

    pending = None
    for i, c in enumerate(order):
        r0 = L * c
        if i + SCORE_LOOKAHEAD < len(order):
            score_phase(order[i + SCORE_LOOKAHEAD])
        for h in range(MLSTM_HEADS):
            slot = c * MLSTM_HEADS + h
            kth = kt_ref[MLSTM_QK_DIM * h:MLSTM_QK_DIM * (h + 1), r0:r0 + L]
            u_r = rowg_ref[urow0 + h:urow0 + h + 1, r0:r0 + L]
            kwt = kth * jnp.exp2(u_r - m_ref[c, h]).astype(BF16)
            dr[c, h] = jnp.dot(kwt, vc_scr[slot, 0:L, 0:MLSTM_V_DIM], preferred_element_type=F32)
            dn[c, h] = jnp.sum(kwt.astype(F32), axis=1, keepdims=True)
        for h in range(MLSTM_HEADS):
            slot = c * MLSTM_HEADS + h
            vc_scr[slot, L:, 0:MLSTM_V_DIM] = c_scr[h].astype(BF16)
            vc_scr[slot, L:, MLSTM_V_DIM:] = n_scr[h].astype(BF16)
            r = jnp.dot(lhs[c, h], vc_scr[slot], preferred_element_type=F32)
            inv = 1.0 / jnp.maximum(jnp.abs(r[:, MLSTM_V_DIM:]), floor[c, h])
            c0 = MLSTM_V_DIM * h
            h_scr[r0:r0 + L, c0:c0 + half] = r[:, :half] * inv
            h_scr[r0:r0 + L, c0 + half:c0 + 2 * half] = r[:, half:2 * half] * inv
            sp = jnp.exp2(m_prev[c, h] - m_ref[c, h])
            c_scr[h] = jnp.concatenate([sp, sp], axis=1) * c_scr[h] + dr[c, h]
            n_scr[h] = sp * n_scr[h] + dn[c, h]
        if chunk_done is not None:
            if pending is not None:
                chunk_done(pending)
            pending = r0
    if pending is not None:
        chunk_done(pending)
    for filler in fillers:
        filler()


def _reset_state(c_scr, n_scr, m_scr):
    c_scr[...] = jnp.zeros(c_scr.shape, F32)
    n_scr[...] = jnp.zeros(n_scr.shape, F32)
    m_scr[...] = jnp.full(m_scr.shape, NEG_INIT, F32)


def _mlstm_in_bwd_kernel(x_ref, g_ref, w_ref, wg_ref, bias_ref, q_ref, kt_ref, v_ref, og_ref, colg_ref, rowg_ref,
                         hb_ref, xn_scr, c_scr, n_scr, m_scr, h_scr, vc_scr):
    @pl.when(pl.program_id(1) == 0)
    def _():
        _reset_state(c_scr, n_scr, m_scr)

    gate_chunks = _mlstm_project(x_ref, g_ref, w_ref, wg_ref, bias_ref, q_ref, kt_ref, v_ref, og_ref, colg_ref,
                                 rowg_ref, xn_scr)
    _mlstm_sweep_tile(q_ref, kt_ref, v_ref, colg_ref, rowg_ref, c_scr, n_scr, m_scr, h_scr, vc_scr, reverse=True,
                      fillers=gate_chunks)
    hb_ref[...] = h_scr[...].astype(BF16)


def _mlstm_fwd_kernel(q_ref, kt_ref, v_ref, colg_ref, rowg_ref, hb_ref, og_ref, hn_ref, x_ref, wo_ref,
                      fg_ref, out_ref, c_scr, n_scr, m_scr, h_scr, vc_scr, *, final_norm):
    @pl.when(pl.program_id(1) == 0)
    def _():
        _reset_state(c_scr, n_scr, m_scr)

    def finish_rows(r0):
        rows = slice(r0, r0 + CHUNK)
        for h in range(MLSTM_HEADS):
            cs = slice(MLSTM_V_DIM * h, MLSTM_V_DIM * (h + 1))
            hh = h_scr[rows, cs] + hb_ref[rows, cs].astype(F32)
            ms = jnp.mean(hh * hh, axis=-1, keepdims=True)
            hh = hh * lax.rsqrt(ms + EPS) * hn_ref[:, cs]
            h_scr[rows, cs] = hh * og_ref[rows, cs].astype(F32)
        y = x_ref[rows, :] + jnp.dot(h_scr[rows, :].astype(BF16), wo_ref[...], preferred_element_type=F32)
        if final_norm:
            ms = jnp.mean(y * y, axis=-1, keepdims=True)
            y = y * lax.rsqrt(ms + EPS) * fg_ref[...]
        out_ref[rows, :] = y

    _mlstm_sweep_tile(q_ref, kt_ref, v_ref, colg_ref, rowg_ref, c_scr, n_scr, m_scr, h_scr, vc_scr, reverse=False,
                      chunk_done=finish_rows)


def _mlstm_state_scratch(tt):
    return [pltpu.VMEM((MLSTM_HEADS, MLSTM_QK_DIM, MLSTM_V_DIM), F32),
            pltpu.VMEM((MLSTM_HEADS, MLSTM_QK_DIM, LANES), F32),
            pltpu.VMEM((GATE_SLAB, LANES), F32),
            pltpu.VMEM((tt, MLSTM_WIDTH), F32),
            pltpu.VMEM((tt // CHUNK * MLSTM_HEADS, CHUNK + MLSTM_QK_DIM, MLSTM_V_DIM + LANES), BF16)]


def _mlstm_in_bwd(x, g, w, wg, bias, seq):
    n = x.shape[0]
    tt = _tile(MLSTM_TILE, seq)
    n_t = seq // tt
    const = lambda b, t: (0, 0)
    row = lambda b, t: (b * n_t + n_t - 1 - t, 0)
    col = lambda b, t: (0, b * n_t + n_t - 1 - t)
    return pl.pallas_call(
        _mlstm_in_bwd_kernel,
        grid=(n // seq, n_t),
        in_specs=[pl.BlockSpec((tt, D_MODEL), row), pl.BlockSpec((1, D_MODEL), const),
                  pl.BlockSpec(w.shape, const, pipeline_mode=pl.Buffered(1)), pl.BlockSpec(wg.shape, const),
                  pl.BlockSpec((1, LANES), const)],
        out_specs=[pl.BlockSpec((tt, MLSTM_QK_WIDTH), row), pl.BlockSpec((MLSTM_QK_WIDTH, tt), col),
                   pl.BlockSpec((tt, MLSTM_WIDTH), row), pl.BlockSpec((tt, MLSTM_WIDTH), row),
                   pl.BlockSpec((tt, LANES), row), pl.BlockSpec((ROW_SLABS * GATE_SLAB, tt), col),
                   pl.BlockSpec((tt, MLSTM_WIDTH), row)],
        out_shape=[jax.ShapeDtypeStruct((n, MLSTM_QK_WIDTH), BF16), jax.ShapeDtypeStruct((MLSTM_QK_WIDTH, n), BF16),
                   jax.ShapeDtypeStruct((n, MLSTM_WIDTH), BF16), jax.ShapeDtypeStruct((n, MLSTM_WIDTH), BF16),
                   jax.ShapeDtypeStruct((n, LANES), F32), jax.ShapeDtypeStruct((ROW_SLABS * GATE_SLAB, n), F32),
                   jax.ShapeDtypeStruct((n, MLSTM_WIDTH), BF16)],
        scratch_shapes=[pltpu.VMEM((tt, D_MODEL), BF16)] + _mlstm_state_scratch(tt),
        compiler_params=_params(2),
        name="mlstm_in_bwd",
    )(x, g, w, wg, bias)


def _mlstm_fwd(q, kt, v, colg, rowg, hb, og, hn, x, wo, fg, seq, final_norm):
    n = x.shape[0]
    tt = _tile(MLSTM_TILE, seq)
    n_t = seq // tt
    const = lambda b, t: (0, 0)
    row = lambda b, t: (b * n_t + t, 0)
    col = lambda b, t: (0, b * n_t + t)
    return pl.pallas_call(
        functools.partial(_mlstm_fwd_kernel, final_norm=final_norm),
        grid=(n // seq, n_t),
        in_specs=[pl.BlockSpec((tt, MLSTM_QK_WIDTH), row), pl.BlockSpec((MLSTM_QK_WIDTH, tt), col),
                  pl.BlockSpec((tt, MLSTM_WIDTH), row), pl.BlockSpec((tt, LANES), row),
                  pl.BlockSpec((ROW_SLABS * GATE_SLAB, tt), col),
                  pl.BlockSpec((tt, MLSTM_WIDTH), row), pl.BlockSpec((tt, MLSTM_WIDTH), row),
                  pl.BlockSpec((1, MLSTM_WIDTH), const), pl.BlockSpec((tt, D_MODEL), row),
                  pl.BlockSpec((MLSTM_WIDTH, D_MODEL), const), pl.BlockSpec((1, D_MODEL), const)],
        out_specs=pl.BlockSpec((tt, D_MODEL), row),
        out_shape=jax.ShapeDtypeStruct((n, D_MODEL), F32),
        scratch_shapes=_mlstm_state_scratch(tt),
        compiler_params=_params(2),
        name="mlstm_fwd",
    )(q, kt, v, colg, rowg, hb, og, hn, x, wo, fg)


def _rope_tables(seq):
    half = ATTN_HEAD_DIM // 2
    inv = jnp.exp(-math.log(ROPE_THETA) * jnp.arange(half, dtype=F32) / half)
    ang = jnp.arange(seq).astype(F32)[:, None] * inv[None, :]
    cos = jnp.cos(ang)
    sin = jnp.sin(ang)
    cos_t = jnp.tile(cos, (1, LANES // half))
    sin_t = jnp.tile(jnp.concatenate([-sin, sin], axis=1), (1, LANES // ATTN_HEAD_DIM))
    return cos_t, sin_t


def _gate_slabs(a):
    lead = a.shape[:-1]
    a = a.reshape(lead + (4, MLSTM_HEADS))
    a = jnp.pad(a, [(0, 0)] * len(lead) + [(0, 0), (0, GATE_SLAB - MLSTM_HEADS)])
    a = a.reshape(lead + (4 * GATE_SLAB,))
    return jnp.pad(a, [(0, 0)] * len(lead) + [(0, LANES - 4 * GATE_SLAB)])


def _trunk(x, p):
    bsz, seq, _ = x.shape
    xf = x.reshape(bsz * seq, D_MODEL)
    cos_t, sin_t = _rope_tables(seq)
    for i in range(4):
        j = i // 2
        g = p["norm_g"][i][None, :]
        if i % 2 == 0:
            q, kv, gate = _attn_in(xf, g, p["attn_w_in"][j], cos_t, sin_t, seq)
            xf = _attn_core(p["attn_sink"][j], q, kv, gate, xf, p["attn_w_out"][j], seq)
        else:
            q, kt, v, og, colg, rowg, hb = _mlstm_in_bwd(xf, g, p["mlstm_w_main"][j], p["mlstm_w_gate"][j],
                                                         p["mlstm_gate_bias"][j], seq)
            xf = _mlstm_fwd(q, kt, v, colg, rowg, hb, og, p["mlstm_head_norm"][j][None, :], xf, p["mlstm_w_out"][j],
                            p["final_norm_g"][None, :], seq, final_norm=(i == 3))
    return xf.reshape(bsz, seq, D_MODEL)


def kernel(x_prompt, x_sample, norm_g, attn_w_in, attn_sink, attn_w_out, mlstm_w_in, mlstm_gate_bias, mlstm_head_norm, mlstm_w_out, final_norm_g):
    p = {
        "norm_g": norm_g,
        "attn_w_in": attn_w_in.astype(BF16),
        "attn_sink": attn_sink,
        "attn_w_out": attn_w_out.astype(BF16),
        "mlstm_w_main": mlstm_w_in[:, :, :MLSTM_MAIN_IN].astype(BF16),
        "mlstm_w_gate": _gate_slabs(mlstm_w_in[:, :, MLSTM_MAIN_IN:]).astype(BF16),
        "mlstm_gate_bias": _gate_slabs(mlstm_gate_bias)[:, None, :],
        "mlstm_head_norm": mlstm_head_norm,
        "mlstm_w_out": mlstm_w_out.astype(BF16),
        "final_norm_g": final_norm_g,
    }
    return _trunk(x_prompt, p), _trunk(x_sample, p)
```

```python
import functools
import math

import jax
import jax.numpy as jnp
from jax import lax
from jax.experimental import pallas as pl
from jax.experimental.pallas import tpu as pltpu

F32 = jnp.float32
BF16 = jnp.bfloat16

D_MODEL = 1024
EPS = 1e-6
NEG = -1e30
LOG2E = math.log2(math.e)

ATTN_HEADS = 16
ATTN_KV_HEADS = 4
ATTN_HEAD_DIM = 64
ATTN_WIDTH = ATTN_HEADS * ATTN_HEAD_DIM
ATTN_KV_WIDTH = ATTN_KV_HEADS * ATTN_HEAD_DIM
WINDOW = 128
ROPE_THETA = 10000.0
KEY_BLOCK = 128

MLSTM_HEADS = 4
MLSTM_V_DIM = 256
MLSTM_QK_DIM = 128
MLSTM_WIDTH = MLSTM_HEADS * MLSTM_V_DIM
MLSTM_QK_WIDTH = MLSTM_HEADS * MLSTM_QK_DIM
MLSTM_MAIN_IN = 2 * MLSTM_QK_WIDTH + 3 * MLSTM_WIDTH
NEG_INIT = -1e30
CHUNK = 128
GATE_SLAB = 8
SCORE_LOOKAHEAD = 2

LANES = 128
PROJ_COLS = 256
VMEM_LIMIT = 56 * 1024 * 1024

PROJ_TILE = 1024
ATTN_TILE = 1024
MLSTM_TILE = 1024


def _params(n_axes):
    return pltpu.CompilerParams(dimension_semantics=("arbitrary",) * n_axes, vmem_limit_bytes=VMEM_LIMIT)


def _tile(preferred, seq):
    tile = min(preferred, seq)
    assert seq % tile == 0 and tile % LANES == 0, (seq, tile)
    return tile


def _normed_bf16(x_ref, g_ref):
    x = x_ref[...]
    ms = jnp.mean(x * x, axis=-1, keepdims=True)
    return (x * lax.rsqrt(ms + EPS) * g_ref[...]).astype(BF16)


def _attn_in_kernel(x_ref, g_ref, w_ref, cos_ref, sin_ref, q_ref, kv_ref, gate_ref, xn_scr):
    tm = x_ref.shape[0]
    xn_scr[...] = _normed_bf16(x_ref, g_ref)
    cos = cos_ref[...]
    sin = sin_ref[...]
    lane = lax.broadcasted_iota(jnp.int32, (tm, LANES), 1)
    first_half = (lane % ATTN_HEAD_DIM) < (ATTN_HEAD_DIM // 2)

    def rope(a):
        partner = jnp.where(first_half, pltpu.roll(a, LANES - 32, 1), pltpu.roll(a, 32, 1))
        return a * cos + partner * sin

    def proj(col, width):
        return jnp.dot(xn_scr[...], w_ref[:, col:col + width], preferred_element_type=F32)

    scale = ATTN_HEAD_DIM ** -0.5 * LOG2E
    for c in range(ATTN_WIDTH // PROJ_COLS):
        acc = proj(PROJ_COLS * c, PROJ_COLS)
        for j in range(2):
            q_ref[:, PROJ_COLS * c + LANES * j:PROJ_COLS * c + LANES * (j + 1)] = (
                rope(acc[:, LANES * j:LANES * (j + 1)]) * scale).astype(BF16)
    acc = proj(ATTN_WIDTH, 2 * ATTN_KV_WIDTH)
    for j in range(ATTN_KV_WIDTH // LANES):
        kv_ref[:, LANES * j:LANES * (j + 1)] = rope(acc[:, LANES * j:LANES * (j + 1)]).astype(BF16)
    kv_ref[:, ATTN_KV_WIDTH:] = acc[:, ATTN_KV_WIDTH:].astype(BF16)
    z0 = ATTN_WIDTH + 2 * ATTN_KV_WIDTH
    for c in range(ATTN_WIDTH // PROJ_COLS):
        z = proj(z0 + PROJ_COLS * c, PROJ_COLS)
        gate_ref[:, PROJ_COLS * c:PROJ_COLS * (c + 1)] = (z * jax.nn.sigmoid(z)).astype(BF16)


def _attn_in(x, g, w, cos_t, sin_t, seq):
    n = x.shape[0]
    tm = _tile(PROJ_TILE, seq)
    tiles_per_seq = seq // tm
    row = lambda i: (i, 0)
    const = lambda i: (0, 0)
    pos = lambda i: (i % tiles_per_seq, 0)
    return pl.pallas_call(
        _attn_in_kernel,
        grid=(n // tm,),
        in_specs=[pl.BlockSpec((tm, D_MODEL), row), pl.BlockSpec((1, D_MODEL), const),
                  pl.BlockSpec(w.shape, const, pipeline_mode=pl.Buffered(1)),
                  pl.BlockSpec((tm, LANES), pos), pl.BlockSpec((tm, LANES), pos)],
        out_specs=[pl.BlockSpec((tm, ATTN_WIDTH), row), pl.BlockSpec((tm, 2 * ATTN_KV_WIDTH), row),
                   pl.BlockSpec((tm, ATTN_WIDTH), row)],
        out_shape=[jax.ShapeDtypeStruct((n, ATTN_WIDTH), BF16), jax.ShapeDtypeStruct((n, 2 * ATTN_KV_WIDTH), BF16),
                   jax.ShapeDtypeStruct((n, ATTN_WIDTH), BF16)],
        scratch_shapes=[pltpu.VMEM((tm, D_MODEL), BF16)],
        compiler_params=_params(1),
        name="attn_in",
    )(x, g, w, cos_t, sin_t)


def _attn_core_kernel(sink_ref, q_ref, kv_ref, kvp_ref, kvn_ref, gate_ref, x_ref, wo_ref, out_ref,
                      kx_scr, vx_scr, o_scr):
    t = pl.program_id(1)
    n_t = pl.num_programs(1)
    tq = q_ref.shape[0]
    nblk = tq // KEY_BLOCK
    kb = KEY_BLOCK
    half = ATTN_HEAD_DIM

    lane = lax.broadcasted_iota(jnp.int32, (kb, LANES), 1)
    low = lane < half
    ones_lo = jnp.where(low, 1.0, 0.0).astype(BF16)
    ones_hi = jnp.where(low, 0.0, 1.0).astype(BF16)

    def expand(blk_ref, r0, jb):
        for vi in range(ATTN_KV_WIDTH // LANES):
            kcol = blk_ref[r0:r0 + kb, LANES * vi:LANES * (vi + 1)].astype(F32)
            vcol = blk_ref[r0:r0 + kb, ATTN_KV_WIDTH + LANES * vi:ATTN_KV_WIDTH + LANES * (vi + 1)].astype(F32)
            kswap = pltpu.roll(kcol, half, 1)
            vswap = pltpu.roll(vcol, half, 1)
            for hf in range(2):
                g = 2 * vi + hf
                k_src_lo, k_src_hi = (kcol, kswap) if hf == 0 else (kswap, kcol)
                v_src_lo, v_src_hi = (vcol, vswap) if hf == 0 else (vswap, vcol)
                kx_scr[g, jb, 0] = jnp.where(low, k_src_lo, 0.0).astype(BF16)
                kx_scr[g, jb, 1] = jnp.where(low, 0.0, k_src_hi).astype(BF16)
                vx_scr[g, jb, 0, :, :LANES] = jnp.where(low, v_src_lo, 0.0).astype(BF16)
                vx_scr[g, jb, 0, :, LANES:] = ones_lo
                vx_scr[g, jb, 1, :, :LANES] = jnp.where(low, 0.0, v_src_hi).astype(BF16)
                vx_scr[g, jb, 1, :, LANES:] = ones_hi

    expand(kvp_ref, 0, 0)
    for jb in range(nblk):
        expand(kv_ref, kb * jb, jb + 1)
    expand(kvn_ref, 0, nblk + 1)

    qi = lax.broadcasted_iota(jnp.int32, (kb, kb), 0)
    ki = lax.broadcasted_iota(jnp.int32, (kb, kb), 1)
    band_prev = jnp.where(ki >= qi, 0.0, NEG).astype(F32)
    band_next = jnp.where(ki <= qi, 0.0, NEG).astype(F32)

    for blk in range(nblk):
        r0 = kb * blk
        bias_prev = jnp.where(t == 0, NEG, band_prev) if blk == 0 else band_prev
        bias_next = jnp.where(t == n_t - 1, NEG, band_next) if blk == nblk - 1 else band_next
        for g in range(ATTN_KV_HEADS):
            c0 = 2 * LANES * g
            q2 = jnp.concatenate([q_ref[r0:r0 + kb, c0:c0 + LANES],
                                  q_ref[r0:r0 + kb, c0 + LANES:c0 + 2 * LANES]], axis=0)
            kx = kx_scr[g, blk:blk + 3].reshape(6 * kb, LANES)
            s = lax.dot_general(q2, kx, (((1,), (1,)), ((), ())), preferred_element_type=F32)
            p_rows = []
            sink_terms = []
            for p in range(2):
                cols = [None] * 6
                maxes = []
                sinks = []
                for ab in range(2):
                    head = 4 * g + 2 * p + ab
                    sink = sink_ref[head] * LOG2E
                    sp = s[kb * p:kb * (p + 1), kb * ab:kb * (ab + 1)] + bias_prev
                    so = s[kb * p:kb * (p + 1), kb * (2 + ab):kb * (3 + ab)]
                    sn = s[kb * p:kb * (p + 1), kb * (4 + ab):kb * (5 + ab)] + bias_next
                    m = jnp.max(jnp.maximum(jnp.maximum(sp, so), sn), axis=1, keepdims=True)
                    m = jnp.maximum(m, sink)
                    cols[ab] = jnp.exp2(sp - m).astype(BF16)
                    cols[2 + ab] = jnp.exp2(so - m).astype(BF16)
                    cols[4 + ab] = jnp.exp2(sn - m).astype(BF16)
                    maxes.append(m)
                    sinks.append(sink)
                p_rows.append(jnp.concatenate(cols, axis=1))
                sink_terms.append(jnp.exp2(jnp.where(low, sinks[0], sinks[1]) - jnp.where(low, maxes[0], maxes[1])))
            pmat = jnp.concatenate(p_rows, axis=0)
            vx = vx_scr[g, blk:blk + 3].reshape(6 * kb, 2 * LANES)
            r = jnp.dot(pmat, vx, preferred_element_type=F32)
            for p in range(2):
                num = r[kb * p:kb * (p + 1), :LANES]
                den = r[kb * p:kb * (p + 1), LANES:] + sink_terms[p]
                o_scr[r0:r0 + kb, c0 + LANES * p:c0 + LANES * (p + 1)] = num / den

    y = (o_scr[...] * gate_ref[...].astype(F32)).astype(BF16)
    out_ref[...] = x_ref[...] + jnp.dot(y, wo_ref[...], preferred_element_type=F32)


def _attn_core(sink, q, kv, gate, x, wo, seq):
    n = x.shape[0]
    tq = _tile(ATTN_TILE, seq)
    n_t = seq // tq
    nblk = tq // KEY_BLOCK
    last_blk = n // KEY_BLOCK - 1
    row = lambda b, t: (b * n_t + t, 0)
    const = lambda b, t: (0, 0)
    prev = lambda b, t: (jnp.maximum((b * n_t + t) * nblk - 1, 0), 0)
    nxt = lambda b, t: (jnp.minimum((b * n_t + t + 1) * nblk, last_blk), 0)
    return pl.pallas_call(
        _attn_core_kernel,
        grid=(n // seq, n_t),
        in_specs=[pl.BlockSpec(memory_space=pltpu.SMEM),
                  pl.BlockSpec((tq, ATTN_WIDTH), row), pl.BlockSpec((tq, 2 * ATTN_KV_WIDTH), row),
                  pl.BlockSpec((KEY_BLOCK, 2 * ATTN_KV_WIDTH), prev), pl.BlockSpec((KEY_BLOCK, 2 * ATTN_KV_WIDTH), nxt),
                  pl.BlockSpec((tq, ATTN_WIDTH), row), pl.BlockSpec((tq, D_MODEL), row),
                  pl.BlockSpec((ATTN_WIDTH, D_MODEL), const)],
        out_specs=pl.BlockSpec((tq, D_MODEL), row),
        out_shape=jax.ShapeDtypeStruct((n, D_MODEL), F32),
        scratch_shapes=[pltpu.VMEM((ATTN_KV_HEADS, nblk + 2, 2, KEY_BLOCK, LANES), BF16),
                        pltpu.VMEM((ATTN_KV_HEADS, nblk + 2, 2, KEY_BLOCK, 2 * LANES), BF16),
                        pltpu.VMEM((tq, ATTN_WIDTH), F32)],
        compiler_params=_params(2),
        name="attn_core",
    )(sink, q, kv, kv, kv, gate, x, wo)


ROW_SLABS = 6


def _log_sigmoid(x):
    return jnp.minimum(x, 0.0) - jnp.log1p(jnp.exp(-jnp.abs(x)))


def _segment_scan(x, op, fill, pos_in_chunk, reverse):
    width = x.shape[1]
    d = 1
    while d < CHUNK:
        if reverse:
            shifted = pltpu.roll(x, width - d, 1)
            valid = pos_in_chunk < CHUNK - d
        else:
            shifted = pltpu.roll(x, d, 1)
            valid = pos_in_chunk >= d
        x = op(x, jnp.where(valid, shifted, fill))
        d *= 2
    return x


def _mlstm_project(x_ref, g_ref, w_ref, wg_ref, bias_ref, q_ref, kt_ref, v_ref, og_ref, colg_ref, rowg_ref, xn_scr):
    tm = x_ref.shape[0]
    xn_scr[...] = _normed_bf16(x_ref, g_ref)

    def proj(col, width):
        return jnp.dot(xn_scr[...], w_ref[:, col:col + width], preferred_element_type=F32)

    gates = jnp.dot(xn_scr[...], wg_ref[...], preferred_element_type=F32) + bias_ref[...]
    gt = gates.T
    s = GATE_SLAB
    ig_f, fg_f, ig_b, fg_b = gt[0:s], gt[s:2 * s], gt[2 * s:3 * s], gt[3 * s:4 * s]
    pos = lax.broadcasted_iota(jnp.int32, (s, tm), 1) % CHUNK
    b_f = _segment_scan(_log_sigmoid(fg_f) * LOG2E, jnp.add, 0.0, pos, False)
    u_f = ig_f * LOG2E - b_f
    cu_f = _segment_scan(u_f, jnp.maximum, -jnp.inf, pos, False)
    b_b = _segment_scan(_log_sigmoid(fg_b) * LOG2E, jnp.add, 0.0, pos, True)
    u_b = ig_b * LOG2E - b_b
    cu_b = _segment_scan(u_b, jnp.maximum, -jnp.inf, pos, True)
    pad = jnp.zeros((LANES - 4 * s, tm), F32)
    colg_ref[...] = jnp.concatenate([-b_f, cu_f, -b_b, cu_b, pad], axis=0).T
    rowg_ref[...] = jnp.concatenate([u_f, u_b, b_f, cu_f, b_b, cu_b], axis=0)

    for c in range(MLSTM_QK_WIDTH // PROJ_COLS):
        q_ref[:, PROJ_COLS * c:PROJ_COLS * (c + 1)] = proj(PROJ_COLS * c, PROJ_COLS).astype(BF16)
    kscale = MLSTM_QK_DIM ** -0.5
    for c in range(MLSTM_QK_WIDTH // PROJ_COLS):
        k = proj(MLSTM_QK_WIDTH + PROJ_COLS * c, PROJ_COLS) * kscale
        kt_ref[PROJ_COLS * c:PROJ_COLS * (c + 1), :] = k.T.astype(BF16)
    v0 = 2 * MLSTM_QK_WIDTH
    for c in range(MLSTM_WIDTH // PROJ_COLS):
        v_ref[:, PROJ_COLS * c:PROJ_COLS * (c + 1)] = proj(v0 + PROJ_COLS * c, PROJ_COLS).astype(BF16)
    o0 = v0 + MLSTM_WIDTH
    z0 = o0 + MLSTM_WIDTH

    def gate_chunk(c):
        o = proj(o0 + PROJ_COLS * c, PROJ_COLS)
        z = proj(z0 + PROJ_COLS * c, PROJ_COLS)
        og_ref[:, PROJ_COLS * c:PROJ_COLS * (c + 1)] = (jax.nn.sigmoid(o) * (z * jax.nn.sigmoid(z))).astype(BF16)

    return [functools.partial(gate_chunk, c) for c in range(MLSTM_WIDTH // PROJ_COLS)]


def _mlstm_sweep_tile(q_ref, kt_ref, v_ref, colg_ref, rowg_ref, c_scr, n_scr, m_scr, h_scr, vc_scr,
                      reverse, chunk_done=None, fillers=()):
    assert CHUNK == LANES
    tt = q_ref.shape[0]
    nch = tt // CHUNK
    L = CHUNK
    ti = lax.broadcasted_iota(jnp.int32, (L, L), 0)
    si = lax.broadcasted_iota(jnp.int32, (L, L), 1)
    causal = (si >= ti) if reverse else (si <= ti)
    col0 = 2 * GATE_SLAB if reverse else 0
    urow0 = GATE_SLAB if reverse else 0
    half = MLSTM_V_DIM // 2
    for slot in range(nch * MLSTM_HEADS):
        vc_scr[slot, 0:L, MLSTM_V_DIM:] = jnp.ones((L, LANES), BF16)

    chunks = range(nch - 1, -1, -1) if reverse else range(nch)

    brow0 = (4 if reverse else 2) * GATE_SLAB
    m_prev, m_ref = {}, {}
    m = m_scr[...]
    for c in chunks:
        end = L * c if reverse else L * c + L - 1
        b_end = jnp.broadcast_to(rowg_ref[brow0:brow0 + GATE_SLAB, end:end + 1], (GATE_SLAB, LANES))
        cu_end = jnp.broadcast_to(rowg_ref[brow0 + GATE_SLAB:brow0 + 2 * GATE_SLAB, end:end + 1], (GATE_SLAB, LANES))
        m_new_minus_b = jnp.maximum(m, cu_end)
        for h in range(MLSTM_HEADS):
            m_prev[c, h] = m[h:h + 1, :]
            m_ref[c, h] = m_new_minus_b[h:h + 1, :]
        m = m_new_minus_b + b_end
    m_scr[...] = m

    fillers = list(fillers)
    lhs, floor, dr, dn = {}, {}, {}, {}

    def score_phase(c):
        r0 = L * c
        cg = colg_ref[r0:r0 + L, :]
        if fillers:
            fillers.pop(0)()
        for h in range(MLSTM_HEADS):
            slot = c * MLSTM_HEADS + h
            qh = q_ref[r0:r0 + L, MLSTM_QK_DIM * h:MLSTM_QK_DIM * (h + 1)]
            kth = kt_ref[MLSTM_QK_DIM * h:MLSTM_QK_DIM * (h + 1), r0:r0 + L]
            nb_rep = jnp.broadcast_to(cg[:, col0 + h:col0 + h + 1], (L, LANES))
            cu_rep = jnp.broadcast_to(cg[:, col0 + GATE_SLAB + h:col0 + GATE_SLAB + h + 1], (L, LANES))
            u_r = rowg_ref[urow0 + h:urow0 + h + 1, r0:r0 + L]
            mx = jnp.maximum(cu_rep, m_prev[c, h])
            e = jnp.exp2(jnp.where(causal, u_r - mx, NEG))
            s = jnp.dot(qh, kth, preferred_element_type=F32)
            sc = jnp.exp2(m_prev[c, h] - mx)
            lhs[c, h] = jnp.concatenate([(e * s).astype(BF16), qh * sc.astype(BF16)], axis=1)
            floor[c, h] = jnp.exp2(nb_rep - mx)
            vc_scr[slot, 0:L, 0:MLSTM_V_DIM] = v_ref[r0:r0 + L, MLSTM_V_DIM * h:MLSTM_V_DIM * (h + 1)]

    order = list(chunks)
    for c in order[:SCORE_LOOKAHEAD]:
        score_phase(c)

    pending = None
    for i, c in enumerate(order):
        r0 = L * c
        if i + SCORE_LOOKAHEAD < len(order):
            score_phase(order[i + SCORE_LOOKAHEAD])
        for h in range(MLSTM_HEADS):
            slot = c * MLSTM_HEADS + h
            kth = kt_ref[MLSTM_QK_DIM * h:MLSTM_QK_DIM * (h + 1), r0:r0 + L]
            u_r = rowg_ref[urow0 + h:urow0 + h + 1, r0:r0 + L]
            kwt = kth * jnp.exp2(u_r - m_ref[c, h]).astype(BF16)
            if reverse:
                dr[c, h] = jnp.dot(kwt, vc_scr[slot, 0:L, 0:MLSTM_V_DIM], preferred_element_type=F32)
                dn[c, h] = jnp.sum(kwt.astype(F32), axis=1, keepdims=True)
            else:
                inc = jnp.dot(kwt, vc_scr[slot, 0:L, :], preferred_element_type=F32)
                dr[c, h] = inc[:, :MLSTM_V_DIM]
                dn[c, h] = inc[:, MLSTM_V_DIM:]
        for h in range(MLSTM_HEADS):
            slot = c * MLSTM_HEADS + h
            vc_scr[slot, L:, 0:MLSTM_V_DIM] = c_scr[h].astype(BF16)
            vc_scr[slot, L:, MLSTM_V_DIM:] = n_scr[h].astype(BF16)
            r = jnp.dot(lhs[c, h], vc_scr[slot], preferred_element_type=F32)
            inv = 1.0 / jnp.maximum(jnp.abs(r[:, MLSTM_V_DIM:]), floor[c, h])
            c0 = MLSTM_V_DIM * h
            h_scr[r0:r0 + L, c0:c0 + half] = r[:, :half] * inv
            h_scr[r0:r0 + L, c0 + half:c0 + 2 * half] = r[:, half:2 * half] * inv
            sp = jnp.exp2(m_prev[c, h] - m_ref[c, h])
            c_scr[h] = jnp.concatenate([sp, sp], axis=1) * c_scr[h] + dr[c, h]
            n_scr[h] = sp * n_scr[h] + dn[c, h]
        if chunk_done is not None:
            if pending is not None:
                chunk_done(pending)
            pending = r0
    if pending is not None:
        chunk_done(pending)
    for filler in fillers:
        filler()


def _reset_state(c_scr, n_scr, m_scr):
    c_scr[...] = jnp.zeros(c_scr.shape, F32)
    n_scr[...] = jnp.zeros(n_scr.shape, F32)
    m_scr[...] = jnp.full(m_scr.shape, NEG_INIT, F32)


def _mlstm_in_bwd_kernel(x_ref, g_ref, w_ref, wg_ref, bias_ref, q_ref, kt_ref, v_ref, og_ref, colg_ref, rowg_ref,
                         hb_ref, xn_scr, c_scr, n_scr, m_scr, h_scr, vc_scr):
    @pl.when(pl.program_id(1) == 0)
    def _():
        _reset_state(c_scr, n_scr, m_scr)

    gate_chunks = _mlstm_project(x_ref, g_ref, w_ref, wg_ref, bias_ref, q_ref, kt_ref, v_ref, og_ref, colg_ref,
                                 rowg_ref, xn_scr)
    _mlstm_sweep_tile(q_ref, kt_ref, v_ref, colg_ref, rowg_ref, c_scr, n_scr, m_scr, h_scr, vc_scr, reverse=True,
                      fillers=gate_chunks)
    hb_ref[...] = h_scr[...].astype(BF16)


def _mlstm_fwd_kernel(q_ref, kt_ref, v_ref, colg_ref, rowg_ref, hb_ref, og_ref, hn_ref, x_ref, wo_ref,
                      fg_ref, out_ref, c_scr, n_scr, m_scr, h_scr, vc_scr, *, final_norm):
    @pl.when(pl.program_id(1) == 0)
    def _():
        _reset_state(c_scr, n_scr, m_scr)

    def finish_rows(r0):
        rows = slice(r0, r0 + CHUNK)
        for h in range(MLSTM_HEADS):
            cs = slice(MLSTM_V_DIM * h, MLSTM_V_DIM * (h + 1))
            hh = h_scr[rows, cs] + hb_ref[rows, cs].astype(F32)
            ms = jnp.mean(hh * hh, axis=-1, keepdims=True)
            hh = hh * lax.rsqrt(ms + EPS) * hn_ref[:, cs]
            h_scr[rows, cs] = hh * og_ref[rows, cs].astype(F32)
        y = x_ref[rows, :] + jnp.dot(h_scr[rows, :].astype(BF16), wo_ref[...], preferred_element_type=F32)
        if final_norm:
            ms = jnp.mean(y * y, axis=-1, keepdims=True)
            y = y * lax.rsqrt(ms + EPS) * fg_ref[...]
        out_ref[rows, :] = y

    _mlstm_sweep_tile(q_ref, kt_ref, v_ref, colg_ref, rowg_ref, c_scr, n_scr, m_scr, h_scr, vc_scr, reverse=False,
                      chunk_done=finish_rows)


def _mlstm_state_scratch(tt):
    return [pltpu.VMEM((MLSTM_HEADS, MLSTM_QK_DIM, MLSTM_V_DIM), F32),
            pltpu.VMEM((MLSTM_HEADS, MLSTM_QK_DIM, LANES), F32),
            pltpu.VMEM((GATE_SLAB, LANES), F32),
            pltpu.VMEM((tt, MLSTM_WIDTH), F32),
            pltpu.VMEM((tt // CHUNK * MLSTM_HEADS, CHUNK + MLSTM_QK_DIM, MLSTM_V_DIM + LANES), BF16)]


def _mlstm_in_bwd(x, g, w, wg, bias, seq):
    n = x.shape[0]
    tt = _tile(MLSTM_TILE, seq)
    n_t = seq // tt
    const = lambda b, t: (0, 0)
    row = lambda b, t: (b * n_t + n_t - 1 - t, 0)
    col = lambda b, t: (0, b * n_t + n_t - 1 - t)
    return pl.pallas_call(
        _mlstm_in_bwd_kernel,
        grid=(n // seq, n_t),
        in_specs=[pl.BlockSpec((tt, D_MODEL), row), pl.BlockSpec((1, D_MODEL), const),
                  pl.BlockSpec(w.shape, const, pipeline_mode=pl.Buffered(1)), pl.BlockSpec(wg.shape, const),
                  pl.BlockSpec((1, LANES), const)],
        out_specs=[pl.BlockSpec((tt, MLSTM_QK_WIDTH), row), pl.BlockSpec((MLSTM_QK_WIDTH, tt), col),
                   pl.BlockSpec((tt, MLSTM_WIDTH), row), pl.BlockSpec((tt, MLSTM_WIDTH), row),
                   pl.BlockSpec((tt, LANES), row), pl.BlockSpec((ROW_SLABS * GATE_SLAB, tt), col),
                   pl.BlockSpec((tt, MLSTM_WIDTH), row)],
        out_shape=[jax.ShapeDtypeStruct((n, MLSTM_QK_WIDTH), BF16), jax.ShapeDtypeStruct((MLSTM_QK_WIDTH, n), BF16),
                   jax.ShapeDtypeStruct((n, MLSTM_WIDTH), BF16), jax.ShapeDtypeStruct((n, MLSTM_WIDTH), BF16),
                   jax.ShapeDtypeStruct((n, LANES), F32), jax.ShapeDtypeStruct((ROW_SLABS * GATE_SLAB, n), F32),
                   jax.ShapeDtypeStruct((n, MLSTM_WIDTH), BF16)],
        scratch_shapes=[pltpu.VMEM((tt, D_MODEL), BF16)] + _mlstm_state_scratch(tt),
        compiler_params=_params(2),
        name="mlstm_in_bwd",
    )(x, g, w, wg, bias)


def _mlstm_fwd(q, kt, v, colg, rowg, hb, og, hn, x, wo, fg, seq, final_norm):
    n = x.shape[0]
    tt = _tile(MLSTM_TILE, seq)
    n_t = seq // tt
    const = lambda b, t: (0, 0)
    row = lambda b, t: (b * n_t + t, 0)
    col = lambda b, t: (0, b * n_t + t)
    return pl.pallas_call(
        functools.partial(_mlstm_fwd_kernel, final_norm=final_norm),
        grid=(n // seq, n_t),
        in_specs=[pl.BlockSpec((tt, MLSTM_QK_WIDTH), row), pl.BlockSpec((MLSTM_QK_WIDTH, tt), col),
                  pl.BlockSpec((tt, MLSTM_WIDTH), row), pl.BlockSpec((tt, LANES), row),
                  pl.BlockSpec((ROW_SLABS * GATE_SLAB, tt), col),
                  pl.BlockSpec((tt, MLSTM_WIDTH), row), pl.BlockSpec((tt, MLSTM_WIDTH), row),
                  pl.BlockSpec((1, MLSTM_WIDTH), const), pl.BlockSpec((tt, D_MODEL), row),
                  pl.BlockSpec((MLSTM_WIDTH, D_MODEL), const), pl.BlockSpec((1, D_MODEL), const)],
        out_specs=pl.BlockSpec((tt, D_MODEL), row),
        out_shape=jax.ShapeDtypeStruct((n, D_MODEL), F32),
        scratch_shapes=_mlstm_state_scratch(tt),
        compiler_params=_params(2),
        name="mlstm_fwd",
    )(q, kt, v, colg, rowg, hb, og, hn, x, wo, fg)


def _rope_tables(seq):
    half = ATTN_HEAD_DIM // 2
    inv = jnp.exp(-math.log(ROPE_THETA) * jnp.arange(half, dtype=F32) / half)
    ang = jnp.arange(seq).astype(F32)[:, None] * inv[None, :]
    cos = jnp.cos(ang)
    sin = jnp.sin(ang)
    cos_t = jnp.tile(cos, (1, LANES // half))
    sin_t = jnp.tile(jnp.concatenate([-sin, sin], axis=1), (1, LANES // ATTN_HEAD_DIM))
    return cos_t, sin_t


def _gate_slabs(a):
    lead = a.shape[:-1]
    a = a.reshape(lead + (4, MLSTM_HEADS))
    a = jnp.pad(a, [(0, 0)] * len(lead) + [(0, 0), (0, GATE_SLAB - MLSTM_HEADS)])
    a = a.reshape(lead + (4 * GATE_SLAB,))
    return jnp.pad(a, [(0, 0)] * len(lead) + [(0, LANES - 4 * GATE_SLAB)])


def _trunk(x, p):
    bsz, seq, _ = x.shape
    xf = x.reshape(bsz * seq, D_MODEL)
    cos_t, sin_t = _rope_tables(seq)
    for i in range(4):
        j = i // 2
        g = p["norm_g"][i][None, :]
        if i % 2 == 0:
            q, kv, gate = _attn_in(xf, g, p["attn_w_in"][j], cos_t, sin_t, seq)
            xf = _attn_core(p["attn_sink"][j], q, kv, gate, xf, p["attn_w_out"][j], seq)
        else:
            q, kt, v, og, colg, rowg, hb = _mlstm_in_bwd(xf, g, p["mlstm_w_main"][j], p["mlstm_w_gate"][j],
                                                         p["mlstm_gate_bias"][j], seq)
            xf = _mlstm_fwd(q, kt, v, colg, rowg, hb, og, p["mlstm_head_norm"][j][None, :], xf, p["mlstm_w_out"][j],
                            p["final_norm_g"][None, :], seq, final_norm=(i == 3))
    return xf.reshape(bsz, seq, D_MODEL)


def kernel(x_prompt, x_sample, norm_g, attn_w_in, attn_sink, attn_w_out, mlstm_w_in, mlstm_gate_bias, mlstm_head_norm, mlstm_w_out, final_norm_g):
    p = {
        "norm_g": norm_g,
        "attn_w_in": attn_w_in.astype(BF16),
        "attn_sink": attn_sink,
        "attn_w_out": attn_w_out.astype(BF16),
        "mlstm_w_main": mlstm_w_in[:, :, :MLSTM_MAIN_IN].astype(BF16),
        "mlstm_w_gate": _gate_slabs(mlstm_w_in[:, :, MLSTM_MAIN_IN:]).astype(BF16),
        "mlstm_gate_bias": _gate_slabs(mlstm_gate_bias)[:, None, :],
        "mlstm_head_norm": mlstm_head_norm,
        "mlstm_w_out": mlstm_w_out.astype(BF16),
        "final_norm_g": final_norm_g,
    }
    return _trunk(x_prompt, p), _trunk(x_sample, p)
```
